```python
import math
import jax, jax.numpy as jnp
from jax import lax
import numpy as np


D_MODEL = 1024
BATCH = 16
SEQ = 2048
DEPTH = 4

GRID_W = 64
CTX_LEN = 256
DIFF_HEADS = D_MODEL // 256
DIFF_QK_DIM = 64
DIFF_V_DIM = 2 * DIFF_QK_DIM
DIFF_WIDTH = DIFF_HEADS * DIFF_V_DIM
GLA_HEADS = D_MODEL // 256
GLA_DK = 64
GLA_DV = 128
GLA_WIDTH = GLA_HEADS * GLA_DV
GLA_GATE_RANK = 16
GLA_GATE_NORMALIZER = 16.0
GLA_CHUNK = 64
MIX_WIDTH = DIFF_WIDTH + GLA_WIDTH
Q_BLOCK = 128
ROPE_THETA = 10000.0
D_FF = 2816
N_EXPERTS = 8
TOP_K = 2
N_DENSE = (DEPTH + 1) // 2
N_MOE = DEPTH // 2
N_MOD = 6
EPS = 1e-6
IN_SIZES = (2 * DIFF_HEADS * DIFF_QK_DIM, 2 * DIFF_HEADS * DIFF_QK_DIM, DIFF_WIDTH, GLA_HEADS * GLA_DK, GLA_HEADS * GLA_DK, GLA_WIDTH, GLA_WIDTH, 2 * GLA_GATE_RANK)
IN_OFFSETS = tuple(int(o) for o in np.cumsum(IN_SIZES)[:-1])
N_IN = int(sum(IN_SIZES))

kernel_name = 'hybrid_diffattn_gla_moe_dit_block'


def rms_norm(x, g):
    xf = x.astype(jnp.float32)
    y = xf * lax.rsqrt(jnp.mean(xf * xf, axis=-1, keepdims=True) + EPS)
    return (y * g.astype(jnp.float32)).astype(x.dtype)


def rope_tables(n_tokens, dtype):
    rows_n = n_tokens // GRID_W
    row = jnp.repeat(jnp.arange(rows_n), GRID_W).astype(jnp.float32)
    col = jnp.tile(jnp.arange(GRID_W), rows_n).astype(jnp.float32)
    n_freq = DIFF_QK_DIM // 4
    inv = ROPE_THETA ** (-jnp.arange(n_freq, dtype=jnp.float32) / n_freq)
    ang_r = row[:, None] * inv
    ang_c = col[:, None] * inv
    return (jnp.cos(ang_r).astype(dtype), jnp.sin(ang_r).astype(dtype),
            jnp.cos(ang_c).astype(dtype), jnp.sin(ang_c).astype(dtype))


def apply_rot(x, cos, sin):
    x1, x2 = jnp.split(x, 2, axis=-1)
    c = cos[None, :, None, :]
    s = sin[None, :, None, :]
    return jnp.concatenate([x1 * c - x2 * s, x1 * s + x2 * c], axis=-1)


def rope_2d(x, tabs):
    cr, sr, cc, sc = tabs
    half = x.shape[-1] // 2
    return jnp.concatenate([apply_rot(x[..., :half], cr, sr), apply_rot(x[..., half:], cc, sc)], axis=-1)


def diff_attn_core(q1, q2, k1, k2, v, lam):
    scale = DIFF_QK_DIM ** -0.5
    s1 = jnp.einsum('bqhd,bkhd->bhqk', q1, k1).astype(jnp.float32) * scale
    s2 = jnp.einsum('bqhd,bkhd->bhqk', q2, k2).astype(jnp.float32) * scale
    a = jax.nn.softmax(s1, axis=-1) - lam * jax.nn.softmax(s2, axis=-1)
    return jnp.einsum('bhqk,bkhe->bqhe', a.astype(v.dtype), v)


def diff_attn_blocked(q1, q2, k1, k2, v, lam):
    b, s, h, d = q1.shape
    nb = s // Q_BLOCK
    to_blocks = lambda q: q.reshape(b, nb, Q_BLOCK, h, d).transpose(1, 0, 2, 3, 4)
    o = lax.map(lambda qq: diff_attn_core(qq[0], qq[1], k1, k2, v, lam), (to_blocks(q1), to_blocks(q2)))
    return o.transpose(1, 0, 2, 3, 4).reshape(b, s, h, v.shape[-1])


def gla_chunked(q, k, v, glog, s0):
    b, s, h, dk = q.shape
    dv = v.shape[-1]
    C = GLA_CHUNK
    nc = s // C
    f32 = jnp.float32
    qc = q.reshape(b, nc, C, h, dk).astype(f32)
    kc = k.reshape(b, nc, C, h, dk).astype(f32)
    vc = v.reshape(b, nc, C, h, dv).astype(f32)
    g = jnp.cumsum(glog.reshape(b, nc, C, h, dk).astype(f32), axis=2)
    g_last = g[:, :, -1:]
    g_mid = g[:, :, C // 2 - 1:C // 2]
    att = jnp.einsum('bnihk,bnjhk->bnhij', qc * jnp.exp(g - g_mid), kc * jnp.exp(g_mid - g))
    att = jnp.where(jnp.tril(jnp.ones((C, C), dtype=bool)), att, 0.0)
    o_intra = jnp.einsum('bnhij,bnjhv->bnihv', att, vc)
    ds = jnp.einsum('bnjhk,bnjhv->bnhkv', kc * jnp.exp(g_last - g), vc)
    decay = jnp.exp(g_last[:, :, 0])

    def step(state, inp):
        dec, d_s = inp
        return dec[..., None] * state + d_s, state

    s_final, s_prev = lax.scan(step, s0.astype(f32), (jnp.moveaxis(decay, 1, 0), jnp.moveaxis(ds, 1, 0)))
    s_prev = jnp.moveaxis(s_prev, 0, 1)
    o_inter = jnp.einsum('bnihk,bnhkv->bnihv', qc * jnp.exp(g), s_prev)
    o = (o_intra + o_inter).reshape(b, s, h, dv).astype(v.dtype)
    return o, s_final


def gla_bidir(q, k, v, gl_f, gl_b, s0_f, s0_b):
    flip = lambda t: t[:, ::-1]
    o_f, s_f = gla_chunked(q, k, v, gl_f, s0_f)
    o_b, s_b = gla_chunked(flip(q), flip(k), flip(v), flip(gl_b), s0_b)
    return o_f + flip(o_b), s_f, s_b


def gla_gates(glr, gate_w2, gate_b):
    b, s = glr.shape[:2]
    z = jnp.einsum('bsdr,drk->bsdk', glr, gate_w2) + gate_b
    gl = (jax.nn.log_sigmoid(z.astype(jnp.float32)) / GLA_GATE_NORMALIZER).reshape(b, s, 2, GLA_HEADS, GLA_DK)
    return gl[:, :, 0], gl[:, :, 1]


def split_proj(p):
    b, s = p.shape[:2]
    dq, dk, dv, gq, gk, gv, gg, glr = jnp.split(p, IN_OFFSETS, axis=-1)
    dq = dq.reshape(b, s, DIFF_HEADS, 2, DIFF_QK_DIM)
    dk = dk.reshape(b, s, DIFF_HEADS, 2, DIFF_QK_DIM)
    dv = dv.reshape(b, s, DIFF_HEADS, DIFF_V_DIM)
    gq = gq.reshape(b, s, GLA_HEADS, GLA_DK) * (GLA_DK ** -0.5)
    gk = gk.reshape(b, s, GLA_HEADS, GLA_DK)
    gv = gv.reshape(b, s, GLA_HEADS, GLA_DV)
    glr = glr.reshape(b, s, 2, GLA_GATE_RANK)
    return dq[..., 0, :], dq[..., 1, :], dk[..., 0, :], dk[..., 1, :], dv, gq, gk, gv, gg, glr


def token_mixer(h_lat, h_ctx, w_in, gate_w2, gate_b, lam_vec, lam_init, subln_g, onorm_g, w_out, rope_tabs, with_ctx_out):
    f32 = jnp.float32
    lv = lam_vec.astype(f32)
    lam = jnp.exp(jnp.sum(lv[0] * lv[1])) - jnp.exp(jnp.sum(lv[2] * lv[3])) + lam_init
    q1l, q2l, k1l, k2l, vl, gql, gkl, gvl, ggl, glrl = split_proj(h_lat @ w_in)
    q1c, q2c, k1c, k2c, vc, gqc, gkc, gvc, ggc, glrc = split_proj(h_ctx @ w_in)
    q1l, q2l, k1l, k2l = (rope_2d(t, rope_tabs) for t in (q1l, q2l, k1l, k2l))
    k1a = jnp.concatenate([k1c, k1l], axis=1)
    k2a = jnp.concatenate([k2c, k2l], axis=1)
    va = jnp.concatenate([vc, vl], axis=1)
    od_lat = diff_attn_blocked(q1l, q2l, k1a, k2a, va, lam)
    gf_c, gb_c = gla_gates(glrc, gate_w2, gate_b)
    gf_l, gb_l = gla_gates(glrl, gate_w2, gate_b)
    s0 = jnp.zeros((h_ctx.shape[0], GLA_HEADS, GLA_DK, GLA_DV), f32)
    og_ctx, s_f, s_b = gla_bidir(gqc, gkc, gvc, gf_c, gb_c, s0, s0)
    og_lat, _, _ = gla_bidir(gql, gkl, gvl, gf_l, gb_l, s_f, s_b)

    def merge(o_diff, o_gla, g_out):
        b, s = o_diff.shape[:2]
        od = (rms_norm(o_diff, subln_g) * (1.0 - lam_init)).reshape(b, s, DIFF_WIDTH)
        og = rms_norm(o_gla, onorm_g).reshape(b, s, GLA_WIDTH) * jax.nn.silu(g_out)
        return jnp.concatenate([od, og], axis=-1) @ w_out

    y_lat = merge(od_lat, og_lat, ggl)
    y_ctx = None
    if with_ctx_out:
        od_ctx = diff_attn_core(q1c, q2c, k1c, k2c, vc, lam)
        y_ctx = merge(od_ctx, og_ctx, ggc)
    return y_lat, y_ctx


def swiglu(h, w1, w3, w2):
    return (jax.nn.silu(h @ w1) * (h @ w3)) @ w2


def moe_ffn(h, router, w1, w3, w2):
    shape = h.shape
    ht = h.reshape(-1, shape[-1])
    logits = (ht @ router).astype(jnp.float32)
    top_v, top_i = lax.top_k(logits, TOP_K)
    wts = jax.nn.softmax(top_v, axis=-1)
    combine = jnp.sum(jax.nn.one_hot(top_i, N_EXPERTS, dtype=jnp.float32) * wts[..., None], axis=1).astype(h.dtype)

    def expert_step(acc, p):
        e1, e3, e2, cw = p
        return acc + cw[:, None] * swiglu(ht, e1, e3, e2), None

    out, _ = lax.scan(expert_step, jnp.zeros_like(ht), (w1, w3, w2, combine.T))
    return out.reshape(shape)


def setup_inputs(seed: int = 0) -> dict:
    key = jax.random.key(seed)
    ks = jax.random.split(key, 24)
    f32 = jnp.float32
    nrm = lambda k, shape, s: jax.random.normal(k, shape, f32) * s
    gain = lambda k, shape: 1.0 + 0.02 * jax.random.normal(k, shape, f32)
    return {
        'x': nrm(ks[0], (BATCH, SEQ, D_MODEL), 1.0),
        'c': nrm(ks[1], (BATCH, D_MODEL), 1.0),
        'ctx': nrm(ks[2], (BATCH, CTX_LEN, D_MODEL), 1.0),
        'c_ctx': nrm(ks[3], (D_MODEL,), 1.0),
        'w_mod': nrm(ks[4], (DEPTH, D_MODEL, N_MOD * D_MODEL), 0.5 * D_MODEL ** -0.5),
        'b_mod': nrm(ks[5], (DEPTH, N_MOD * D_MODEL), 0.02),
        'g_pre_mix': gain(ks[6], (DEPTH, D_MODEL)),
        'g_post_mix': gain(ks[7], (DEPTH, D_MODEL)),
        'g_pre_ffn': gain(ks[8], (DEPTH, D_MODEL)),
        'g_post_ffn': gain(ks[9], (DEPTH, D_MODEL)),
        'w_in': nrm(ks[10], (DEPTH, D_MODEL, N_IN), D_MODEL ** -0.5),
        'gla_gate_w2': nrm(ks[11], (DEPTH, 2, GLA_GATE_RANK, GLA_HEADS * GLA_DK), GLA_GATE_RANK ** -0.5),
        'gla_gate_b': nrm(ks[12], (DEPTH, 2, GLA_HEADS * GLA_DK), 0.1),
        'diff_lambda': nrm(ks[13], (DEPTH, 4, DIFF_QK_DIM), 0.1),
        'diff_subln': gain(ks[14], (DEPTH, DIFF_V_DIM)),
        'gla_onorm': gain(ks[15], (DEPTH, GLA_DV)),
        'w_out': nrm(ks[16], (DEPTH, MIX_WIDTH, D_MODEL), MIX_WIDTH ** -0.5),
        'ffn_w1': nrm(ks[17], (N_DENSE, D_MODEL, D_FF), D_MODEL ** -0.5),
        'ffn_w3': nrm(ks[18], (N_DENSE, D_MODEL, D_FF), D_MODEL ** -0.5),
        'ffn_w2': nrm(ks[19], (N_DENSE, D_FF, D_MODEL), D_FF ** -0.5),
        'router': nrm(ks[20], (N_MOE, D_MODEL, N_EXPERTS), D_MODEL ** -0.5),
        'moe_w1': nrm(ks[21], (N_MOE, N_EXPERTS, D_MODEL, D_FF), D_MODEL ** -0.5),
        'moe_w3': nrm(ks[22], (N_MOE, N_EXPERTS, D_MODEL, D_FF), D_MODEL ** -0.5),
        'moe_w2': nrm(ks[23], (N_MOE, N_EXPERTS, D_FF, D_MODEL), D_FF ** -0.5),
    }


def reference(x, c, ctx, c_ctx, w_mod, b_mod, g_pre_mix, g_post_mix, g_pre_ffn, g_post_ffn, w_in, gla_gate_w2, gla_gate_b, diff_lambda, diff_subln, gla_onorm, w_out, ffn_w1, ffn_w3, ffn_w2, router, moe_w1, moe_w3, moe_w2):
    rope_tabs = rope_tables(x.shape[1], x.dtype)
    silu_c = jax.nn.silu(c)
    silu_cc = jax.nn.silu(c_ctx)
    xc = ctx
    for l in range(DEPTH):
        last = l == DEPTH - 1
        lam_init = 0.8 - 0.6 * math.exp(-0.3 * l)
        sh1, sc1, gt1, sh2, sc2, gt2 = (m[:, None] for m in jnp.split(silu_c @ w_mod[l] + b_mod[l], N_MOD, axis=-1))
        csh1, csc1, cgt1, csh2, csc2, cgt2 = jnp.split(silu_cc @ w_mod[l] + b_mod[l], N_MOD, axis=-1)
        h = rms_norm(x, g_pre_mix[l]) * (1.0 + sc1) + sh1
        hc = rms_norm(xc, g_pre_mix[l]) * (1.0 + csc1) + csh1
        y, yc = token_mixer(h, hc, w_in[l], gla_gate_w2[l], gla_gate_b[l], diff_lambda[l], lam_init,
                            diff_subln[l], gla_onorm[l], w_out[l], rope_tabs, not last)
        x = x + gt1 * rms_norm(y, g_post_mix[l])
        if not last:
            xc = xc + cgt1 * rms_norm(yc, g_post_mix[l])
        if l % 2 == 0:
            ffn = lambda t, i=l // 2: swiglu(t, ffn_w1[i], ffn_w3[i], ffn_w2[i])
        else:
            ffn = lambda t, i=l // 2: moe_ffn(t, router[i], moe_w1[i], moe_w3[i], moe_w2[i])
        h = rms_norm(x, g_pre_ffn[l]) * (1.0 + sc2) + sh2
        x = x + gt2 * rms_norm(ffn(h), g_post_ffn[l])
        if not last:
            hc = rms_norm(xc, g_pre_ffn[l]) * (1.0 + csc2) + csh2
            xc = xc + cgt2 * rms_norm(ffn(hc), g_post_ffn[l])
    return x
```

```python
import functools
import math

import jax
import jax.numpy as jnp
import numpy as np
from jax import lax
from jax.experimental import pallas as pl
from jax.experimental.pallas import tpu as pltpu

F32 = jnp.float32
BF16 = jnp.bfloat16
HIGHEST = lax.Precision.HIGHEST

N_HEADS = 4
QK_DIM = 64
V_DIM = 128
HEAD_W = N_HEADS * V_DIM
GLA_KW = N_HEADS * QK_DIM
GATE_RANK = 16
GATE_NORMALIZER = 16.0
GLA_CHUNK = 64
GRID_W = 64
ROPE_THETA = 10000.0
N_MOD = 6
TOP_K = 2
EPS = 1e-6
LANES = 128
VMEM_LIMIT = 56 * 1024 * 1024

_OFF_DQ, _OFF_DK, _OFF_DV = 0, 512, 1024
_OFF_GQ, _OFF_GK, _OFF_GV, _OFF_GG, _OFF_GLR = 1536, 1792, 2048, 2560, 3072


def _rms(x, g):
    ms = jnp.mean(x * x, axis=-1, keepdims=True)
    return x * lax.rsqrt(ms + EPS) * g


def _silu(x):
    return x * (1.0 / (1.0 + jnp.exp(-x)))


def _cparams(n_axes, vmem=None):
    return pltpu.CompilerParams(dimension_semantics=("arbitrary",) * n_axes,
                                vmem_limit_bytes=vmem)


def _mod_kernel(c_ref, w_ref, b_ref, o_ref):
    o_ref[...] = jnp.dot(_silu(c_ref[...]), w_ref[...], precision=HIGHEST,
                         preferred_element_type=F32) + b_ref[...]


def _modulation(cc, w_mod, b_mod):
    n_layers, d, _ = w_mod.shape
    r = cc.shape[0]
    b4 = b_mod.reshape(n_layers, N_MOD, 1, d)
    return pl.pallas_call(
        _mod_kernel,
        out_shape=jax.ShapeDtypeStruct((n_layers, N_MOD, r, d), F32),
        grid=(n_layers, N_MOD),
        in_specs=[pl.BlockSpec((r, d), lambda l, j: (0, 0)),
                  pl.BlockSpec((None, d, d), lambda l, j: (l, 0, j)),
                  pl.BlockSpec((None, None, 1, d), lambda l, j: (l, j, 0, 0))],
        out_specs=pl.BlockSpec((None, None, r, d), lambda l, j: (l, j, 0, 0)),
        compiler_params=_cparams(2),
        name="modulation",
    )(cc, w_mod, b4)


def _inproj_kernel(x_ref, g_ref, sc_ref, sh_ref, w_ref, wglr_ref, w2_ref, gb_ref,
                   cos_ref, s1_ref, s2_ref,
                   dq_ref, dk_ref, dv_ref, gq_ref, gk_ref, gv_ref, gg_ref, glf_ref, glb_ref):
    h = _rms(x_ref[...], g_ref[...]) * (1.0 + sc_ref[...]) + sh_ref[...]
    hb = h.astype(BF16)
    cos, s1, s2 = cos_ref[...], s1_ref[...], s2_ref[...]

    def proj(off, n):
        return jnp.dot(hb, w_ref[:, off:off + n], preferred_element_type=F32)

    def rope(p):
        outs = []
        for j in range(N_HEADS):
            xg = p[:, LANES * j:LANES * (j + 1)]
            outs.append(xg * cos + pltpu.roll(xg, LANES - 16, 1) * s1 + pltpu.roll(xg, 16, 1) * s2)
        return jnp.concatenate(outs, axis=1)

    scale = QK_DIM ** -0.5
    dq_ref[...] = (rope(proj(_OFF_DQ, HEAD_W)) * scale).astype(BF16)
    dk_ref[...] = rope(proj(_OFF_DK, HEAD_W)).astype(BF16)
    dv_ref[...] = proj(_OFF_DV, HEAD_W).astype(BF16)
    gq_ref[...] = proj(_OFF_GQ, GLA_KW) * scale
    gk_ref[...] = proj(_OFF_GK, GLA_KW)
    gv_ref[...] = proj(_OFF_GV, HEAD_W)
    gg_ref[...] = proj(_OFF_GG, HEAD_W)
    glr = jnp.dot(hb, wglr_ref[...], preferred_element_type=F32)
    z = jnp.dot(glr, w2_ref[...], precision=HIGHEST, preferred_element_type=F32) + gb_ref[...]
    gl = (jnp.minimum(z, 0.0) - jnp.log1p(jnp.exp(-jnp.abs(z)))) * (1.0 / GATE_NORMALIZER)
    glf_ref[...] = gl[:, :GLA_KW]
    glb_ref[...] = gl[:, GLA_KW:]


def _inproj(xall, g_pre, mod5, layer, w_main, w_glr, w2bd, gate_b, rope_tabs, geom):
    t, d = xall.shape
    tm, tiles_per_b, ctx_row = geom
    n_tiles = t // tm

    def mod_spec(j):
        return pl.BlockSpec((None, None, None, 1, d),
                            lambda i: (layer, j, _mod_row(i, tiles_per_b, ctx_row), 0, 0))

    row = lambda w: pl.BlockSpec((tm, w), lambda i: (i, 0))
    full = lambda a: pl.BlockSpec(a.shape, lambda i: (0,) * a.ndim)
    tab = pl.BlockSpec((tm, LANES), lambda i: (i % tiles_per_b, 0))
    widths = (HEAD_W, HEAD_W, HEAD_W, GLA_KW, GLA_KW, HEAD_W, HEAD_W, GLA_KW, GLA_KW)
    dtypes = (BF16, BF16, BF16, F32, F32, F32, F32, F32, F32)
    return pl.pallas_call(
        _inproj_kernel,
        out_shape=[jax.ShapeDtypeStruct((t, w), dt) for w, dt in zip(widths, dtypes)],
        grid=(n_tiles,),
        in_specs=[row(d), full(g_pre), mod_spec(1), mod_spec(0), full(w_main), full(w_glr),
                  full(w2bd), full(gate_b), tab, tab, tab],
        out_specs=[row(w) for w in widths],
        compiler_params=_cparams(1, VMEM_LIMIT),
        name="inproj",
    )(xall, g_pre, mod5, mod5, w_main, w_glr, w2bd, gate_b, *rope_tabs)


def _mod_row(i, tiles_per_b, ctx_row):
    return jnp.where(i % tiles_per_b == 0, ctx_row, i // tiles_per_b)


def _diff_attn_kernel(q_ref, k_ref, v_ref, lam_ref, g_ref, o_ref, *, lam_init, ctx_len, with_ctx):
    lv = lam_ref[...]
    lam = (jnp.exp(jnp.sum(lv[0:1] * lv[1:2], axis=-1, keepdims=True))
           - jnp.exp(jnp.sum(lv[2:3] * lv[3:4], axis=-1, keepdims=True)) + lam_init)
    lane = lax.broadcasted_iota(jnp.int32, (1, LANES), 1)
    q = q_ref[...]
    q1 = jnp.where(lane < QK_DIM, q, jnp.zeros_like(q))
    q2 = jnp.where(lane >= QK_DIM, q, jnp.zeros_like(q))

    def attend(k, v):
        def one(qm):
            s = lax.dot_general(qm, k, (((1,), (1,)), ((), ())), preferred_element_type=F32)
            p = jnp.exp(s - jnp.max(s, axis=-1, keepdims=True))
            o = jnp.dot(p.astype(BF16), v, preferred_element_type=F32)
            return o / jnp.sum(p, axis=-1, keepdims=True)
        o = one(q1) - lam * one(q2)
        o_ref[...] = (_rms(o, g_ref[...]) * (1.0 - lam_init)).astype(o_ref.dtype)

    is_ctx = pl.program_id(2) == 0

    @pl.when(jnp.logical_not(is_ctx))
    def _():
        attend(k_ref[...], v_ref[...])

    @pl.when(is_ctx)
    def _():
        if with_ctx:
            attend(k_ref[0:ctx_len, :], v_ref[0:ctx_len, :])
        else:
            o_ref[...] = jnp.zeros_like(o_ref)


def _diff_attn(dq, dk, dv, lam_vec, subln_g, lam_init, with_ctx, geom, batch):
    t = dq.shape[0]
    tm, tiles_per_b, _ = geom
    seg = tm * tiles_per_b
    dk3 = dk.reshape(batch, seg, HEAD_W)
    dv3 = dv.reshape(batch, seg, HEAD_W)
    kern = functools.partial(_diff_attn_kernel, lam_init=lam_init, ctx_len=tm, with_ctx=with_ctx)
    return pl.pallas_call(
        kern,
        out_shape=jax.ShapeDtypeStruct((t, HEAD_W), BF16),
        grid=(batch, N_HEADS, tiles_per_b),
        in_specs=[pl.BlockSpec((tm, V_DIM), lambda b, h, i: (b * tiles_per_b + i, h)),
                  pl.BlockSpec((None, seg, V_DIM), lambda b, h, i: (b, 0, h)),
                  pl.BlockSpec((None, seg, V_DIM), lambda b, h, i: (b, 0, h)),
                  pl.BlockSpec(lam_vec.shape, lambda b, h, i: (0, 0)),
                  pl.BlockSpec(subln_g.shape, lambda b, h, i: (0, 0))],
        out_specs=pl.BlockSpec((tm, V_DIM), lambda b, h, i: (b * tiles_per_b + i, h)),
        compiler_params=_cparams(3, VMEM_LIMIT),
        name="diff_attn",
    )(dq, dk3, dv3, lam_vec, subln_g)


def _gla_kernel(q_ref, k_ref, v_ref, glf_ref, glb_ref, o_ref, *, n_chunks, n_ctx_chunks):
    c = GLA_CHUNK
    r_i = lax.broadcasted_iota(jnp.int32, (c, c), 0)
    c_i = lax.broadcasted_iota(jnp.int32, (c, c), 1)
    lane = lax.broadcasted_iota(jnp.int32, (1, GLA_KW), 1)
    head_masks = [((lane // QK_DIM) == h).astype(F32) for h in range(N_HEADS)]
    blk = (lax.broadcasted_iota(jnp.int32, (HEAD_W, GLA_KW), 0) // V_DIM
           == lax.broadcasted_iota(jnp.int32, (HEAD_W, GLA_KW), 1) // QK_DIM).astype(F32)

    for direction in range(2):
        gl_ref = glf_ref if direction == 0 else glb_ref
        if direction == 0:
            tri = (r_i >= c_i)
            last, mid = c - 1, c // 2 - 1
        else:
            tri = (c_i >= r_i)
            last, mid = 0, c - c // 2
        tri_f = tri.astype(F32)

        def chunk_of(i):
            if direction == 0:
                return i
            return jnp.where(i < n_ctx_chunks, n_ctx_chunks - 1 - i, n_chunks - 1 - (i - n_ctx_chunks))

        def body(i, st):
            r0 = pl.multiple_of(chunk_of(i) * c, c)
            rows = pl.ds(r0, c)
            q, k, v, gl = q_ref[rows, :], k_ref[rows, :], v_ref[rows, :], gl_ref[rows, :]
            g = jnp.dot(tri_f, gl, precision=HIGHEST, preferred_element_type=F32)
            g_last = g[last:last + 1, :]
            g_mid = g[mid:mid + 1, :]
            qe = q * jnp.exp(g - g_mid)
            ke = (k * jnp.exp(g_mid - g)).astype(BF16)
            kd = (k * jnp.exp(g_last - g)).astype(BF16)
            qg = (q * jnp.exp(g)).astype(BF16)
            vb = v.astype(BF16)
            o = lax.dot_general(qg, st.astype(BF16), (((1,), (1,)), ((), ())),
                                preferred_element_type=F32)
            outs = []
            for h in range(N_HEADS):
                qh = (qe * head_masks[h]).astype(BF16)
                att = lax.dot_general(qh, ke, (((1,), (1,)), ((), ())), preferred_element_type=F32)
                att = jnp.where(tri, att, 0.0).astype(BF16)
                outs.append(jnp.dot(att, vb[:, V_DIM * h:V_DIM * (h + 1)], preferred_element_type=F32))
            o = o + jnp.concatenate(outs, axis=1)
            if direction == 0:
                o_ref[rows, :] = o
            else:
                o_ref[rows, :] = o_ref[rows, :] + o
            ds_t = lax.dot_general(vb, kd, (((0,), (0,)), ((), ())), preferred_element_type=F32)
            return st * jnp.exp(g_last) + ds_t * blk

        lax.fori_loop(0, n_chunks, body, jnp.zeros((HEAD_W, GLA_KW), F32))


def _gla(gq, gk, gv, glf, glb, geom, batch):
    tm, tiles_per_b, _ = geom
    seg = tm * tiles_per_b
    r3 = lambda a: a.reshape(batch, seg, a.shape[-1])
    spec = lambda w: pl.BlockSpec((None, seg, w), lambda b: (b, 0, 0))
    kern = functools.partial(_gla_kernel, n_chunks=seg // GLA_CHUNK, n_ctx_chunks=tm // GLA_CHUNK)
    out = pl.pallas_call(
        kern,
        out_shape=jax.ShapeDtypeStruct((batch, seg, HEAD_W), F32),
        grid=(batch,),
        in_specs=[spec(GLA_KW), spec(GLA_KW), spec(HEAD_W), spec(GLA_KW), spec(GLA_KW)],
        out_specs=spec(HEAD_W),
        compiler_params=_cparams(1, VMEM_LIMIT),
        name="gla",
    )(r3(gq), r3(gk), r3(gv), r3(glf), r3(glb))
    return out.reshape(batch * seg, HEAD_W)


def _merge_kernel(od_ref, og_ref, gg_ref, x_ref, wout_ref, onorm_ref, gpost_ref, gt_ref,
                  gpre_ref, sc_ref, sh_ref, *rest, n_experts):
    if n_experts:
        router_ref, xo_ref, h_ref, route_ref = rest
    else:
        xo_ref, h_ref = rest
    og = og_ref[...]
    onorm = onorm_ref[...]
    ogn = jnp.concatenate([_rms(og[:, V_DIM * h:V_DIM * (h + 1)], onorm) for h in range(N_HEADS)], axis=1)
    ogn = (ogn * _silu(gg_ref[...])).astype(BF16)
    y = (jnp.dot(od_ref[...], wout_ref[0:HEAD_W, :], preferred_element_type=F32)
         + jnp.dot(ogn, wout_ref[HEAD_W:2 * HEAD_W, :], preferred_element_type=F32))
    x_new = x_ref[...] + gt_ref[...] * _rms(y, gpost_ref[...])
    xo_ref[...] = x_new
    h = _rms(x_new, gpre_ref[...]) * (1.0 + sc_ref[...]) + sh_ref[...]
    h_ref[...] = h.astype(h_ref.dtype)
    if n_experts:
        logits = jnp.dot(h, router_ref[...], precision=HIGHEST, preferred_element_type=F32)
        lane = lax.broadcasted_iota(jnp.int32, logits.shape, 1)
        neg = jnp.float32(-jnp.inf)
        lg = jnp.where(lane < n_experts, logits, neg)
        m1 = jnp.max(lg, axis=-1, keepdims=True)
        i1 = jnp.min(jnp.where(lg == m1, lane, LANES), axis=-1, keepdims=True)
        lg2 = jnp.where(lane == i1, neg, lg)
        m2 = jnp.max(lg2, axis=-1, keepdims=True)
        i2 = jnp.min(jnp.where(lg2 == m2, lane, LANES), axis=-1, keepdims=True)
        e2 = jnp.exp(m2 - m1)
        w1 = 1.0 / (1.0 + e2)
        w2 = e2 / (1.0 + e2)
        route_ref[...] = jnp.where(lane == 0, i1.astype(F32),
                                   jnp.where(lane == 1, i2.astype(F32),
                                             jnp.where(lane == 2, w1, jnp.where(lane == 3, w2, 0.0))))


def _merge(od, og, gg, xall, w_out, onorm, g_post, g_pre_ffn, mod5, layer, router_pad, geom, h_dtype):
    t, d = xall.shape
    tm, tiles_per_b, ctx_row = geom
    n_experts = 0 if router_pad is None else router_pad[1]

    def mod_spec(j):
        return pl.BlockSpec((None, None, None, 1, d),
                            lambda i: (layer, j, _mod_row(i, tiles_per_b, ctx_row), 0, 0))

    row = lambda w: pl.BlockSpec((tm, w), lambda i: (i, 0))
    full = lambda a: pl.BlockSpec(a.shape, lambda i: (0,) * a.ndim)
    in_specs = [row(HEAD_W), row(HEAD_W), row(HEAD_W), row(d), full(w_out), full(onorm), full(g_post),
                mod_spec(2), full(g_pre_ffn), mod_spec(4), mod_spec(3)]
    args = [od, og, gg, xall, w_out, onorm, g_post, mod5, g_pre_ffn, mod5, mod5]
    out_shape = [jax.ShapeDtypeStruct((t, d), F32), jax.ShapeDtypeStruct((t, d), h_dtype)]
    out_specs = [row(d), row(d)]
    if n_experts:
        in_specs.append(full(router_pad[0]))
        args.append(router_pad[0])
        out_shape.append(jax.ShapeDtypeStruct((t, LANES), F32))
        out_specs.append(row(LANES))
    return pl.pallas_call(
        functools.partial(_merge_kernel, n_experts=n_experts),
        out_shape=out_shape,
        grid=(t // tm,),
        in_specs=in_specs,
        out_specs=out_specs,
        compiler_params=_cparams(1, VMEM_LIMIT),
        name="merge_outproj",
    )(*args)


def _swiglu(xb, w1_ref, w3_ref, w2_ref):
    d_ff = w1_ref.shape[-1]
    half = d_ff // 2
    y = None
    for c0 in (0, half):
        a = jnp.dot(xb, w1_ref[:, c0:c0 + half], preferred_element_type=F32)
        b = jnp.dot(xb, w3_ref[:, c0:c0 + half], preferred_element_type=F32)
        m = (_silu(a) * b).astype(BF16)
        part = jnp.dot(m, w2_ref[c0:c0 + half, :], preferred_element_type=F32)
        y = part if y is None else y + part
    return y


def _dense_ffn_kernel(h_ref, x_ref, w1_ref, w3_ref, w2_ref, gpost_ref, gt_ref, o_ref):
    y = _swiglu(h_ref[...], w1_ref, w3_ref, w2_ref)
    o_ref[...] = x_ref[...] + gt_ref[...] * _rms(y, gpost_ref[...])


def _dense_ffn(h, xall, w1, w3, w2, g_post, mod5, layer, geom):
    t, d = xall.shape
    tm, tiles_per_b, ctx_row = geom
    row = lambda w: pl.BlockSpec((tm, w), lambda i: (i, 0))
    resident = lambda a: pl.BlockSpec(a.shape, lambda i: (0,) * a.ndim, pipeline_mode=pl.Buffered(1))
    gt_spec = pl.BlockSpec((None, None, None, 1, d),
                           lambda i: (layer, 5, _mod_row(i, tiles_per_b, ctx_row), 0, 0))
    return pl.pallas_call(
        _dense_ffn_kernel,
        out_shape=jax.ShapeDtypeStruct((t, d), F32),
        grid=(t // tm,),
        in_specs=[row(d), row(d), resident(w1), resident(w3), resident(w2),
                  pl.BlockSpec(g_post.shape, lambda i: (0, 0)), gt_spec],
        out_specs=row(d),
        compiler_params=_cparams(1, VMEM_LIMIT),
        name="dense_ffn",
    )(h, xall, w1, w3, w2, g_post, mod5)


def _row_copy(src_hbm, src_row, dst_buf, dst_row, sem):
    return pltpu.make_async_copy(src_hbm.at[pl.ds(src_row, 1), :], dst_buf.at[pl.ds(dst_row, 1), :], sem)


def _moe_ffn_kernel(te_ref, tv_ref, rows_ref, h_hbm, w1_ref, w3_ref, w2_ref, y_ref, xbuf, sem):
    i = pl.program_id(0)
    tm = xbuf.shape[0]

    @pl.when(tv_ref[i] == 1)
    def _():
        def start(r, carry):
            _row_copy(h_hbm, rows_ref[0, 0, r], xbuf, r, sem.at[0]).start()
            return carry

        def wait(r, carry):
            _row_copy(h_hbm, 0, xbuf, r, sem.at[0]).wait()
            return carry

        lax.fori_loop(0, tm, start, 0)
        lax.fori_loop(0, tm, wait, 0)
        y_ref[...] = _swiglu(xbuf[...].astype(BF16), w1_ref, w3_ref, w2_ref)

    @pl.when(tv_ref[i] == 0)
    def _():
        y_ref[...] = jnp.zeros_like(y_ref)


def _moe_ffn(h, tile_expert, tile_valid, row_token, w1, w3, w2, tm):
    t, d = h.shape
    n_tiles = tile_expert.shape[0]
    d_ff = w1.shape[-1]
    grid_spec = pltpu.PrefetchScalarGridSpec(
        num_scalar_prefetch=2,
        grid=(n_tiles,),
        in_specs=[pl.BlockSpec((1, 1, tm), lambda i, te, tv: (i, 0, 0), memory_space=pltpu.SMEM),
                  pl.BlockSpec(memory_space=pl.ANY),
                  pl.BlockSpec((None, d, d_ff), lambda i, te, tv: (te[i], 0, 0)),
                  pl.BlockSpec((None, d, d_ff), lambda i, te, tv: (te[i], 0, 0)),
                  pl.BlockSpec((None, d_ff, d), lambda i, te, tv: (te[i], 0, 0))],
        out_specs=pl.BlockSpec((tm, d), lambda i, te, tv: (i, 0)),
        scratch_shapes=[pltpu.VMEM((tm, d), F32), pltpu.SemaphoreType.DMA((1,))],
    )
    return pl.pallas_call(
        _moe_ffn_kernel,
        out_shape=jax.ShapeDtypeStruct((n_tiles * tm, d), F32),
        grid_spec=grid_spec,
        compiler_params=_cparams(1, VMEM_LIMIT),
        name="moe_ffn",
    )(tile_expert, tile_valid, row_token.reshape(n_tiles, 1, tm), h, w1, w3, w2)


def _moe_combine_kernel(pos_ref, y_hbm, route_ref, x_ref, gpost_ref, gt_ref, o_ref, ybuf, sem):
    tm = x_ref.shape[0]

    def start(r, carry):
        _row_copy(y_hbm, pos_ref[0, 0, r], ybuf, r, sem.at[0]).start()
        return carry

    def wait(r, carry):
        _row_copy(y_hbm, 0, ybuf, r, sem.at[0]).wait()
        return carry

    lax.fori_loop(0, TOP_K * tm, start, 0)
    lax.fori_loop(0, TOP_K * tm, wait, 0)
    route = route_ref[...]
    y = route[:, 2:3] * ybuf[0:tm, :] + route[:, 3:4] * ybuf[tm:2 * tm, :]
    o_ref[...] = x_ref[...] + gt_ref[...] * _rms(y, gpost_ref[...])


def _moe_combine(y_sorted, pos, route, xall, g_post, mod5, layer, geom):
    t, d = xall.shape
    tm, tiles_per_b, ctx_row = geom
    n_tiles = t // tm
    row = lambda w: pl.BlockSpec((tm, w), lambda i: (i, 0))
    gt_spec = pl.BlockSpec((None, None, None, 1, d),
                           lambda i: (layer, 5, _mod_row(i, tiles_per_b, ctx_row), 0, 0))
    return pl.pallas_call(
        _moe_combine_kernel,
        out_shape=jax.ShapeDtypeStruct((t, d), F32),
        grid=(n_tiles,),
        in_specs=[pl.BlockSpec((1, 1, TOP_K * tm), lambda i: (i, 0, 0), memory_space=pltpu.SMEM),
                  pl.BlockSpec(memory_space=pl.ANY),
                  row(LANES), row(d), pl.BlockSpec(g_post.shape, lambda i: (0, 0)), gt_spec],
        out_specs=row(d),
        scratch_shapes=[pltpu.VMEM((TOP_K * tm, d), F32), pltpu.SemaphoreType.DMA((1,))],
        compiler_params=_cparams(1, VMEM_LIMIT),
        name="moe_combine",
    )(pos, y_sorted, route, xall, g_post, mod5)


def _route_plan(route, tm, n_experts):
    t = route.shape[0]
    n_tiles = (TOP_K * t) // tm + n_experts
    e_tk = route[:, :TOP_K].astype(jnp.int32)
    e_flat = e_tk.T.reshape(-1)
    onehot = (e_flat[:, None] == jnp.arange(n_experts, dtype=jnp.int32)[None, :]).astype(jnp.int32)
    csum = jnp.cumsum(onehot, axis=0)
    counts = csum[-1]
    rank = jnp.sum(csum * onehot, axis=1) - 1
    group_tiles = (counts + tm - 1) // tm
    group_end = jnp.cumsum(group_tiles)
    group_start = (group_end - group_tiles) * tm
    pos = jnp.sum(onehot * group_start[None, :], axis=1) + rank
    tile_group = jnp.sum((jnp.arange(n_tiles, dtype=jnp.int32)[:, None] >= group_end[None, :]).astype(jnp.int32), axis=1)
    tile_valid = (tile_group < n_experts).astype(jnp.int32)
    tile_expert = jnp.minimum(tile_group, n_experts - 1)
    token = jnp.tile(jnp.arange(t, dtype=jnp.int32), TOP_K)
    row_token = jnp.zeros((n_tiles * tm,), jnp.int32).at[pos].set(token, unique_indices=True)
    pos_tiles = pos.reshape(TOP_K, t // tm, tm).transpose(1, 0, 2).reshape(t // tm, 1, TOP_K * tm)
    return tile_expert, tile_valid, row_token, pos_tiles


def _rope_tables(seq, ctx_len):
    rows_n = seq // GRID_W
    row = jnp.repeat(jnp.arange(rows_n), GRID_W).astype(F32)
    col = jnp.tile(jnp.arange(GRID_W), rows_n).astype(F32)
    n_freq = QK_DIM // 4
    inv = ROPE_THETA ** (-jnp.arange(n_freq, dtype=F32) / n_freq)
    ang = jnp.concatenate([row[:, None] * inv, row[:, None] * inv, col[:, None] * inv, col[:, None] * inv], axis=1)
    cos, sin = jnp.cos(ang), jnp.sin(ang)
    first = (jnp.arange(QK_DIM) % (2 * n_freq)) < n_freq
    s1 = jnp.where(first[None, :], -sin, 0.0)
    s2 = jnp.where(first[None, :], 0.0, sin)
    pad = lambda a, v: jnp.concatenate([jnp.full((ctx_len, LANES), v, F32), jnp.tile(a, (1, 2))], axis=0)
    return pad(cos, 1.0), pad(s1, 0.0), pad(s2, 0.0)


def kernel(x, c, ctx, c_ctx, w_mod, b_mod, g_pre_mix, g_post_mix, g_pre_ffn, g_post_ffn, w_in, gla_gate_w2, gla_gate_b, diff_lambda, diff_subln, gla_onorm, w_out, ffn_w1, ffn_w3, ffn_w2, router, moe_w1, moe_w3, moe_w2):
    batch, seq, d = x.shape
    ctx_len = ctx.shape[1]
    depth = w_in.shape[0]
    n_experts = router.shape[-1]
    assert seq % ctx_len == 0 and ctx_len % GLA_CHUNK == 0 and d == 2 * HEAD_W
    assert w_in.shape[-1] == _OFF_GLR + 2 * GATE_RANK
    tm = ctx_len
    tiles_per_b = (ctx_len + seq) // tm
    mod_rows = -(-(batch + 1) // 8) * 8
    geom = (tm, tiles_per_b, batch)

    cc = jnp.concatenate([c, c_ctx[None, :], jnp.zeros((mod_rows - batch - 1, d), F32)], axis=0)
    mod = _modulation(cc, w_mod, b_mod)
    mod5 = mod.reshape(depth, N_MOD, mod_rows, 1, d)
    rope_tabs = _rope_tables(seq, ctx_len)
    xall = jnp.concatenate([ctx, x], axis=1).reshape(batch * (ctx_len + seq), d)
    row2 = lambda a: a.reshape(1, -1)

    for l in range(depth):
        last = l == depth - 1
        lam_init = 0.8 - 0.6 * math.exp(-0.3 * l)
        w_main = w_in[l, :, :_OFF_GLR].astype(BF16)
        w_glr = jnp.pad(w_in[l, :, _OFF_GLR:], ((0, 0), (0, LANES - 2 * GATE_RANK))).astype(BF16)
        w2bd = jnp.zeros((LANES, 2 * GLA_KW), F32)
        w2bd = w2bd.at[:GATE_RANK, :GLA_KW].set(gla_gate_w2[l, 0])
        w2bd = w2bd.at[GATE_RANK:2 * GATE_RANK, GLA_KW:].set(gla_gate_w2[l, 1])
        gate_b = gla_gate_b[l].reshape(1, 2 * GLA_KW)

        dq, dk, dv, gq, gk, gv, gg, glf, glb = _inproj(
            xall, row2(g_pre_mix[l]), mod5, l, w_main, w_glr, w2bd, gate_b, rope_tabs, geom)
        od = _diff_attn(dq, dk, dv, diff_lambda[l], row2(diff_subln[l]), lam_init, not last, geom, batch)
        og = _gla(gq, gk, gv, glf, glb, geom, batch)

        moe = l % 2 == 1
        idx = l // 2
        router_pad = None
        if moe:
            router_pad = (jnp.pad(router[idx], ((0, 0), (0, LANES - n_experts))), n_experts)
        outs = _merge(od, og, gg, xall, w_out[l].astype(BF16), row2(gla_onorm[l]), row2(g_post_mix[l]),
                      row2(g_pre_ffn[l]), mod5, l, router_pad, geom, F32 if moe else BF16)
        if moe:
            xall, h, route = outs
            tile_expert, tile_valid, row_token, pos_tiles = _route_plan(route, tm, n_experts)
            y_sorted = _moe_ffn(h, tile_expert, tile_valid, row_token, moe_w1[idx].astype(BF16),
                                moe_w3[idx].astype(BF16), moe_w2[idx].astype(BF16), tm)
            xall = _moe_combine(y_sorted, pos_tiles, route, xall, row2(g_post_ffn[l]), mod5, l, geom)
        else:
            xall, h = outs
            xall = _dense_ffn(h, xall, ffn_w1[idx].astype(BF16), ffn_w3[idx].astype(BF16),
                              ffn_w2[idx].astype(BF16), row2(g_post_ffn[l]), mod5, l, geom)

    return xall.reshape(batch, ctx_len + seq, d)[:, ctx_len:, :]
```

```python
import functools
import math

import jax
import jax.numpy as jnp
from jax import lax
from jax.experimental import pallas as pl
from jax.experimental.pallas import tpu as pltpu

F32 = jnp.float32
BF16 = jnp.bfloat16
HIGHEST = lax.Precision.HIGHEST

N_HEADS = 4
QK_DIM = 64
V_DIM = 128
HEAD_W = N_HEADS * V_DIM
GLA_KW = N_HEADS * QK_DIM
GATE_RANK = 16
GATE_NORMALIZER = 16.0
GLA_CHUNK = 64
GRID_W = 64
ROPE_THETA = 10000.0
N_MOD = 6
TOP_K = 2
EPS = 1e-6
LOG2_E = 1.4426950408889634
LANES = 128
SUBLANES = 8
VMEM_LIMIT = 56 * 1024 * 1024
DMA_ISSUE_UNROLL = 8

_OFF_DQ, _OFF_DK, _OFF_DV = 0, 512, 1024
_OFF_GQ, _OFF_GK, _OFF_GV, _OFF_GG, _OFF_GLR = 1536, 1792, 2048, 2560, 3072


def _rms(x, g):
    ms = jnp.mean(x * x, axis=-1, keepdims=True)
    return x * lax.rsqrt(ms + EPS) * g


def _silu(x):
    return x * (1.0 / (1.0 + jnp.exp(-x)))


def _cparams(n_axes, vmem=None):
    return pltpu.CompilerParams(dimension_semantics=("arbitrary",) * n_axes,
                                vmem_limit_bytes=vmem)


def _mod_row(i, tiles_per_b, ctx_row):
    return jnp.where(i % tiles_per_b == 0, ctx_row, i // tiles_per_b)


def _to_tiles(ref, val):
    rows = val.shape[0]
    for c in range(SUBLANES):
        ref[pl.ds(c, rows, stride=SUBLANES), :] = val[:, LANES * c:LANES * (c + 1)]


def _from_tiles(ref, rows, row0=0):
    return jnp.concatenate(
        [ref[pl.ds(row0 * SUBLANES + c, rows, stride=SUBLANES), :] for c in range(SUBLANES)], axis=1)


def _mod_kernel(c_ref, w_ref, b_ref, o_ref):
    o_ref[...] = jnp.dot(_silu(c_ref[...]), w_ref[...], precision=HIGHEST,
                         preferred_element_type=F32) + b_ref[...]


def _modulation(cc, w_mod, b_mod):
    n_layers, d, _ = w_mod.shape
    r = cc.shape[0]
    b4 = b_mod.reshape(n_layers, N_MOD, 1, d)
    return pl.pallas_call(
        _mod_kernel,
        out_shape=jax.ShapeDtypeStruct((n_layers, N_MOD, r, d), F32),
        grid=(n_layers, N_MOD),
        in_specs=[pl.BlockSpec((r, d), lambda l, j: (0, 0)),
                  pl.BlockSpec((None, d, d), lambda l, j: (l, 0, j)),
                  pl.BlockSpec((None, None, 1, d), lambda l, j: (l, j, 0, 0))],
        out_specs=pl.BlockSpec((None, None, r, d), lambda l, j: (l, j, 0, 0)),
        compiler_params=_cparams(2),
        name="modulation",
    )(cc, w_mod, b4)


def _inproj_kernel(x_ref, g_ref, sc_ref, sh_ref, w_ref, wglr_ref, w2_ref, gb_ref,
                   cos_ref, s1_ref, s2_ref,
                   dq_ref, dk_ref, dv_ref, gq_ref, gk_ref, gv_ref, gg_ref, gf_ref, gb_out_ref):
    h = _rms(x_ref[...], g_ref[...]) * (1.0 + sc_ref[...]) + sh_ref[...]
    hb = h.astype(BF16)
    cos, s1, s2 = cos_ref[...], s1_ref[...], s2_ref[...]

    def proj(off, n):
        return jnp.dot(hb, w_ref[:, off:off + n], preferred_element_type=F32)

    def rope(p):
        outs = []
        for j in range(N_HEADS):
            xg = p[:, LANES * j:LANES * (j + 1)]
            outs.append(xg * cos + pltpu.roll(xg, LANES - 16, 1) * s1 + pltpu.roll(xg, 16, 1) * s2)
        return jnp.concatenate(outs, axis=1)

    scale = QK_DIM ** -0.5
    dq_ref[...] = (rope(proj(_OFF_DQ, HEAD_W)) * (scale * LOG2_E)).astype(BF16)
    dk_ref[...] = rope(proj(_OFF_DK, HEAD_W)).astype(BF16)
    dv_ref[...] = proj(_OFF_DV, HEAD_W).astype(BF16)
    gq_ref[...] = proj(_OFF_GQ, GLA_KW) * scale
    gk_ref[...] = proj(_OFF_GK, GLA_KW)
    gv_ref[...] = proj(_OFF_GV, HEAD_W)
    gg_ref[...] = proj(_OFF_GG, HEAD_W)
    glr = jnp.dot(hb, wglr_ref[...], preferred_element_type=F32)
    z = jnp.dot(glr, w2_ref[...], precision=HIGHEST, preferred_element_type=F32) + gb_ref[...]
    gl = (jnp.minimum(z, 0.0) - jnp.log1p(jnp.exp(-jnp.abs(z)))) * (1.0 / GATE_NORMALIZER)
    tm = gl.shape[0]
    r_i = lax.broadcasted_iota(jnp.int32, (tm, tm), 0)
    c_i = lax.broadcasted_iota(jnp.int32, (tm, tm), 1)
    same_chunk = (r_i // GLA_CHUNK) == (c_i // GLA_CHUNK)
    lower = jnp.where(same_chunk & (r_i >= c_i), 1.0, 0.0).astype(BF16)
    upper = jnp.where(same_chunk & (c_i >= r_i), 1.0, 0.0).astype(BF16)

    def tri_sum(tri, a):
        hi = a.astype(BF16)
        r1 = a - hi.astype(F32)
        mid = r1.astype(BF16)
        lo = (r1 - mid.astype(F32)).astype(BF16)
        return (jnp.dot(tri, hi, preferred_element_type=F32) + jnp.dot(tri, mid, preferred_element_type=F32)
                + jnp.dot(tri, lo, preferred_element_type=F32))

    gf_ref[...] = tri_sum(lower, gl[:, :GLA_KW])
    gb_out_ref[...] = tri_sum(upper, gl[:, GLA_KW:])


def _inproj(xall, g_pre, mod5, layer, w_main, w_glr, w2bd, gate_b, rope_tabs, geom):
    t, d = xall.shape
    tm, tiles_per_b, ctx_row = geom
    n_tiles = t // tm

    def mod_spec(j):
        return pl.BlockSpec((None, None, None, 1, d),
                            lambda i: (layer, j, _mod_row(i, tiles_per_b, ctx_row), 0, 0))

    row = lambda w: pl.BlockSpec((tm, w), lambda i: (i, 0))
    full = lambda a: pl.BlockSpec(a.shape, lambda i: (0,) * a.ndim)
    tab = pl.BlockSpec((tm, LANES), lambda i: (i % tiles_per_b, 0))
    widths = (HEAD_W, HEAD_W, HEAD_W, GLA_KW, GLA_KW, HEAD_W, HEAD_W, GLA_KW, GLA_KW)
    dtypes = (BF16, BF16, BF16, F32, F32, F32, F32, F32, F32)
    return pl.pallas_call(
        _inproj_kernel,
        out_shape=[jax.ShapeDtypeStruct((t, w), dt) for w, dt in zip(widths, dtypes)],
        grid=(n_tiles,),
        in_specs=[row(d), full(g_pre), mod_spec(1), mod_spec(0), full(w_main), full(w_glr),
                  full(w2bd), full(gate_b), tab, tab, tab],
        out_specs=[row(w) for w in widths],
        compiler_params=_cparams(1, VMEM_LIMIT),
        name="inproj",
    )(xall, g_pre, mod5, mod5, w_main, w_glr, w2bd, gate_b, *rope_tabs)


ATTN_HEADS_PER_STEP = 2


def _diff_attn_kernel(q_ref, k_ref, v_ref, lam_ref, g_ref, o_ref, *, lam_init, ctx_len, with_ctx):
    lv = lam_ref[...]
    lam = (jnp.exp(jnp.sum(lv[0:1] * lv[1:2], axis=-1, keepdims=True))
           - jnp.exp(jnp.sum(lv[2:3] * lv[3:4], axis=-1, keepdims=True)) + lam_init)
    lane = lax.broadcasted_iota(jnp.int32, (1, LANES), 1)

    def attend(n_keys):
        ones = jnp.ones((n_keys, V_DIM), BF16)
        for hh in range(ATTN_HEADS_PER_STEP):
            cols = slice(V_DIM * hh, V_DIM * (hh + 1))
            q = q_ref[:, cols]
            k = k_ref[0:n_keys, cols]
            v_ext = jnp.concatenate([v_ref[0:n_keys, cols], ones], axis=1)

            def one(qm):
                s = lax.dot_general(qm, k, (((1,), (1,)), ((), ())), preferred_element_type=F32)
                p = jnp.exp2(s - jnp.max(s, axis=-1, keepdims=True))
                r = jnp.dot(p.astype(BF16), v_ext, preferred_element_type=F32)
                return r[:, :V_DIM] / r[:, V_DIM:V_DIM + 1]

            o = (one(jnp.where(lane < QK_DIM, q, jnp.zeros_like(q)))
                 - lam * one(jnp.where(lane >= QK_DIM, q, jnp.zeros_like(q))))
            o_ref[:, cols] = (_rms(o, g_ref[...]) * (1.0 - lam_init)).astype(o_ref.dtype)

    is_ctx = pl.program_id(2) == 0

    @pl.when(jnp.logical_not(is_ctx))
    def _():
        attend(k_ref.shape[0])

    @pl.when(is_ctx)
    def _():
        if with_ctx:
            attend(ctx_len)
        else:
            o_ref[...] = jnp.zeros_like(o_ref)


def _diff_attn(dq, dk, dv, lam_vec, subln_g, lam_init, with_ctx, geom, batch):
    t = dq.shape[0]
    tm, tiles_per_b, _ = geom
    seg = tm * tiles_per_b
    w = ATTN_HEADS_PER_STEP * V_DIM
    dk3 = dk.reshape(batch, seg, HEAD_W)
    dv3 = dv.reshape(batch, seg, HEAD_W)
    kern = functools.partial(_diff_attn_kernel, lam_init=lam_init, ctx_len=tm, with_ctx=with_ctx)
    return pl.pallas_call(
        kern,
        out_shape=jax.ShapeDtypeStruct((t, HEAD_W), BF16),
        grid=(batch, N_HEADS // ATTN_HEADS_PER_STEP, tiles_per_b),
        in_specs=[pl.BlockSpec((tm, w), lambda b, h, i: (b * tiles_per_b + i, h)),
                  pl.BlockSpec((None, seg, w), lambda b, h, i: (b, 0, h)),
                  pl.BlockSpec((None, seg, w), lambda b, h, i: (b, 0, h)),
                  pl.BlockSpec(lam_vec.shape, lambda b, h, i: (0, 0)),
                  pl.BlockSpec(subln_g.shape, lambda b, h, i: (0, 0))],
        out_specs=pl.BlockSpec((tm, w), lambda b, h, i: (b * tiles_per_b + i, h)),
        compiler_params=_cparams(3, VMEM_LIMIT),
        name="diff_attn",
    )(dq, dk3, dv3, lam_vec, subln_g)


GLA_UNROLL = 2


def _gla_kernel(q_ref, k_ref, v_ref, gf_ref, gb_ref, o_ref, *, n_chunks, n_ctx_chunks):
    c = GLA_CHUNK
    r_i = lax.broadcasted_iota(jnp.int32, (N_HEADS * c, c), 0) % c
    c_i = lax.broadcasted_iota(jnp.int32, (N_HEADS * c, c), 1)
    lane = lax.broadcasted_iota(jnp.int32, (1, GLA_KW), 1)
    head_masks = [((lane // QK_DIM) == h).astype(F32) for h in range(N_HEADS)]
    blk = (lax.broadcasted_iota(jnp.int32, (HEAD_W, GLA_KW), 0) // V_DIM
           == lax.broadcasted_iota(jnp.int32, (HEAD_W, GLA_KW), 1) // QK_DIM).astype(F32)
    causal = (r_i >= c_i, c_i >= r_i)
    last_row = (c - 1, 0)
    mid_row = (c // 2 - 1, c - c // 2)
    g_refs = (gf_ref, gb_ref)

    def chunk(direction, chunk_idx, st):
        rows = pl.ds(pl.multiple_of(chunk_idx * c, c), c)
        q, k, v, g = q_ref[rows, :], k_ref[rows, :], v_ref[rows, :], g_refs[direction][rows, :]
        last, mid = last_row[direction], mid_row[direction]
        g_last = g[last:last + 1, :]
        g_mid = g[mid:mid + 1, :]
        qe = q * jnp.exp(g - g_mid)
        ke = (k * jnp.exp(g_mid - g)).astype(BF16)
        kd = (k * jnp.exp(g_last - g)).astype(BF16)
        qg = (q * jnp.exp(g)).astype(BF16)
        vb = v.astype(BF16)
        o = lax.dot_general(qg, st.astype(BF16), (((1,), (1,)), ((), ())), preferred_element_type=F32)
        q_stack = jnp.concatenate([(qe * head_masks[h]).astype(BF16) for h in range(N_HEADS)], axis=0)
        att = lax.dot_general(q_stack, ke, (((1,), (1,)), ((), ())), preferred_element_type=F32)
        att = jnp.where(causal[direction], att, 0.0).astype(BF16)
        o = o + jnp.concatenate(
            [jnp.dot(att[c * h:c * (h + 1), :], vb[:, V_DIM * h:V_DIM * (h + 1)], preferred_element_type=F32)
             for h in range(N_HEADS)], axis=1)
        o_ref[rows, :] = o_ref[rows, :] + o
        ds_t = lax.dot_general(vb, kd, (((0,), (0,)), ((), ())), preferred_element_type=F32)
        return st * jnp.exp(g_last) + ds_t * blk

    o_ref[...] = jnp.zeros_like(o_ref)

    def body(i, carry):
        st_f, st_b = carry
        i_b = jnp.where(i < n_ctx_chunks, n_ctx_chunks - 1 - i, n_chunks - 1 - (i - n_ctx_chunks))
        return chunk(0, i, st_f), chunk(1, i_b, st_b)

    zero = jnp.zeros((HEAD_W, GLA_KW), F32)
    lax.fori_loop(0, n_chunks, body, (zero, zero), unroll=GLA_UNROLL)


def _gla(gq, gk, gv, gf, gb, geom, batch):
    tm, tiles_per_b, _ = geom
    seg = tm * tiles_per_b
    r3 = lambda a: a.reshape(batch, seg, a.shape[-1])
    spec = lambda w: pl.BlockSpec((None, seg, w), lambda b: (b, 0, 0))
    kern = functools.partial(_gla_kernel, n_chunks=seg // GLA_CHUNK, n_ctx_chunks=tm // GLA_CHUNK)
    out = pl.pallas_call(
        kern,
        out_shape=jax.ShapeDtypeStruct((batch, seg, HEAD_W), F32),
        grid=(batch,),
        in_specs=[spec(GLA_KW), spec(GLA_KW), spec(HEAD_W), spec(GLA_KW), spec(GLA_KW)],
        out_specs=spec(HEAD_W),
        compiler_params=_cparams(1, VMEM_LIMIT),
        name="gla",
    )(r3(gq), r3(gk), r3(gv), r3(gf), r3(gb))
    return out.reshape(batch * seg, HEAD_W)


def _merge_kernel(od_ref, og_ref, gg_ref, x_ref, wout_ref, onorm_ref, gpost_ref, gt_ref,
                  gpre_ref, sc_ref, sh_ref, *rest, n_experts):
    if n_experts:
        router_ref, xo_ref, h_ref, route_ref = rest
    else:
        xo_ref, h_ref = rest
    og = og_ref[...]
    onorm = onorm_ref[...]
    ogn = jnp.concatenate([_rms(og[:, V_DIM * h:V_DIM * (h + 1)], onorm) for h in range(N_HEADS)], axis=1)
    ogn = (ogn * _silu(gg_ref[...])).astype(BF16)
    y = (jnp.dot(od_ref[...], wout_ref[0:HEAD_W, :], preferred_element_type=F32)
         + jnp.dot(ogn, wout_ref[HEAD_W:2 * HEAD_W, :], preferred_element_type=F32))
    x_new = x_ref[...] + gt_ref[...] * _rms(y, gpost_ref[...])
    xo_ref[...] = x_new
    h = _rms(x_new, gpre_ref[...]) * (1.0 + sc_ref[...]) + sh_ref[...]
    if not n_experts:
        h_ref[...] = h.astype(h_ref.dtype)
        return
    _to_tiles(h_ref, h)
    def split2(a):
        hi = a.astype(BF16)
        return hi, (a - hi.astype(F32)).astype(BF16)

    h_hi, h_lo = split2(h)
    r_hi, r_lo = split2(router_ref[...])
    logits = (jnp.dot(h_hi, r_hi, preferred_element_type=F32) + jnp.dot(h_hi, r_lo, preferred_element_type=F32)
              + jnp.dot(h_lo, r_hi, preferred_element_type=F32))
    lane = lax.broadcasted_iota(jnp.int32, logits.shape, 1)
    neg = jnp.float32(-jnp.inf)
    lg = jnp.where(lane < n_experts, logits, neg)
    m1 = jnp.max(lg, axis=-1, keepdims=True)
    i1 = jnp.min(jnp.where(lg == m1, lane, LANES), axis=-1, keepdims=True)
    lg2 = jnp.where(lane == i1, neg, lg)
    m2 = jnp.max(lg2, axis=-1, keepdims=True)
    i2 = jnp.min(jnp.where(lg2 == m2, lane, LANES), axis=-1, keepdims=True)
    e2 = jnp.exp(m2 - m1)
    w1 = 1.0 / (1.0 + e2)
    w2 = e2 / (1.0 + e2)
    route_ref[...] = jnp.where(lane == 0, i1.astype(F32),
                               jnp.where(lane == 1, i2.astype(F32),
                                         jnp.where(lane == 2, w1, jnp.where(lane == 3, w2, 0.0))))


def _merge(od, og, gg, xall, w_out, onorm, g_post, g_pre_ffn, mod5, layer, router_pad, geom):
    t, d = xall.shape
    tm, tiles_per_b, ctx_row = geom
    n_experts = 0 if router_pad is None else router_pad[1]

    def mod_spec(j):
        return pl.BlockSpec((None, None, None, 1, d),
                            lambda i: (layer, j, _mod_row(i, tiles_per_b, ctx_row), 0, 0))

    row = lambda w: pl.BlockSpec((tm, w), lambda i: (i, 0))
    full = lambda a: pl.BlockSpec(a.shape, lambda i: (0,) * a.ndim)
    in_specs = [row(HEAD_W), row(HEAD_W), row(HEAD_W), row(d), full(w_out), full(onorm), full(g_post),
                mod_spec(2), full(g_pre_ffn), mod_spec(4), mod_spec(3)]
    args = [od, og, gg, xall, w_out, onorm, g_post, mod5, g_pre_ffn, mod5, mod5]
    out_shape = [jax.ShapeDtypeStruct((t, d), F32)]
    out_specs = [row(d)]
    if n_experts:
        in_specs.append(full(router_pad[0]))
        args.append(router_pad[0])
        out_shape += [jax.ShapeDtypeStruct((t * SUBLANES, LANES), F32), jax.ShapeDtypeStruct((t, LANES), F32)]
        out_specs += [pl.BlockSpec((tm * SUBLANES, LANES), lambda i: (i, 0)), row(LANES)]
    else:
        out_shape.append(jax.ShapeDtypeStruct((t, d), BF16))
        out_specs.append(row(d))
    return pl.pallas_call(
        functools.partial(_merge_kernel, n_experts=n_experts),
        out_shape=out_shape,
        grid=(t // tm,),
        in_specs=in_specs,
        out_specs=out_specs,
        compiler_params=_cparams(1, VMEM_LIMIT),
        name="merge_outproj",
    )(*args)


def _swiglu(xb, w1_ref, w3_ref, w2_ref):
    d_ff = w1_ref.shape[-1]
    half = d_ff // 2
    y = None
    for c0 in (0, half):
        a = jnp.dot(xb, w1_ref[:, c0:c0 + half], preferred_element_type=F32)
        b = jnp.dot(xb, w3_ref[:, c0:c0 + half], preferred_element_type=F32)
        m = (_silu(a) * b).astype(BF16)
        part = jnp.dot(m, w2_ref[c0:c0 + half, :], preferred_element_type=F32)
        y = part if y is None else y + part
    return y


def _dense_ffn_kernel(h_ref, x_ref, w1_ref, w3_ref, w2_ref, gpost_ref, gt_ref, o_ref):
    y = _swiglu(h_ref[...], w1_ref, w3_ref, w2_ref)
    o_ref[...] = x_ref[...] + gt_ref[...] * _rms(y, gpost_ref[...])


def _dense_ffn(h, xall, w1, w3, w2, g_post, mod5, layer, geom):
    t, d = xall.shape
    tm, tiles_per_b, ctx_row = geom
    row = lambda w: pl.BlockSpec((tm, w), lambda i: (i, 0))
    resident = lambda a: pl.BlockSpec(a.shape, lambda i: (0,) * a.ndim, pipeline_mode=pl.Buffered(1))
    gt_spec = pl.BlockSpec((None, None, None, 1, d),
                           lambda i: (layer, 5, _mod_row(i, tiles_per_b, ctx_row), 0, 0))
    return pl.pallas_call(
        _dense_ffn_kernel,
        out_shape=jax.ShapeDtypeStruct((t, d), F32),
        grid=(t // tm,),
        in_specs=[row(d), row(d), resident(w1), resident(w3), resident(w2),
                  pl.BlockSpec(g_post.shape, lambda i: (0, 0)), gt_spec],
        out_specs=row(d),
        compiler_params=_cparams(1, VMEM_LIMIT),
        name="dense_ffn",
    )(h, xall, w1, w3, w2, g_post, mod5)


def _tile_copy(src, src_tok, dst, dst_tok, sem):
    s0 = pl.multiple_of(src_tok * SUBLANES, SUBLANES)
    d0 = pl.multiple_of(dst_tok * SUBLANES, SUBLANES)
    return pltpu.make_async_copy(src.at[pl.ds(s0, SUBLANES), :], dst.at[pl.ds(d0, SUBLANES), :], sem)


def _gather_tiles(src_hbm, idx_ref, idx_next_ref, buf, sem, n_copies):
    i = pl.program_id(0)
    slot = lax.rem(i, 2)

    def issue(ref, s):
        def start(r, carry):
            _tile_copy(src_hbm, ref[0, 0, r], buf.at[s], r, sem.at[s]).start()
            return carry
        lax.fori_loop(0, n_copies, start, 0, unroll=DMA_ISSUE_UNROLL)

    @pl.when(i == 0)
    def _():
        issue(idx_ref, 0)

    @pl.when(i + 1 < pl.num_programs(0))
    def _():
        issue(idx_next_ref, 1 - slot)

    def wait(r, carry):
        _tile_copy(src_hbm, 0, buf.at[slot], r, sem.at[slot]).wait()
        return carry

    lax.fori_loop(0, n_copies, wait, 0, unroll=DMA_ISSUE_UNROLL)
    return buf.at[slot]


def _moe_ffn_kernel(te_ref, tv_ref, rows_ref, rows_next_ref, h_hbm, w1_ref, w3_ref, w2_ref, y_ref,
                    xbuf, sem, *, tm):
    del te_ref
    xb = _gather_tiles(h_hbm, rows_ref, rows_next_ref, xbuf, sem, tm)
    valid = tv_ref[pl.program_id(0)] == 1

    @pl.when(valid)
    def _():
        _to_tiles(y_ref, _swiglu(_from_tiles(xb, tm).astype(BF16), w1_ref, w3_ref, w2_ref))

    @pl.when(jnp.logical_not(valid))
    def _():
        y_ref[...] = jnp.zeros_like(y_ref)


def _moe_ffn(h_tiles, tile_expert, tile_valid, row_token, w1, w3, w2, tm):
    n_tiles = tile_expert.shape[0]
    _, d, d_ff = w1.shape
    rows = row_token.reshape(n_tiles, 1, tm)
    idx_spec = lambda f: pl.BlockSpec((1, 1, tm), lambda i, te, tv: (f(i), 0, 0), memory_space=pltpu.SMEM)
    grid_spec = pltpu.PrefetchScalarGridSpec(
        num_scalar_prefetch=2,
        grid=(n_tiles,),
        in_specs=[idx_spec(lambda i: i), idx_spec(lambda i: jnp.minimum(i + 1, n_tiles - 1)),
                  pl.BlockSpec(memory_space=pl.ANY),
                  pl.BlockSpec((None, d, d_ff), lambda i, te, tv: (te[i], 0, 0)),
                  pl.BlockSpec((None, d, d_ff), lambda i, te, tv: (te[i], 0, 0)),
                  pl.BlockSpec((None, d_ff, d), lambda i, te, tv: (te[i], 0, 0))],
        out_specs=pl.BlockSpec((tm * SUBLANES, LANES), lambda i, te, tv: (i, 0)),
        scratch_shapes=[pltpu.VMEM((2, tm * SUBLANES, LANES), F32), pltpu.SemaphoreType.DMA((2,))],
    )
    return pl.pallas_call(
        functools.partial(_moe_ffn_kernel, tm=tm),
        out_shape=jax.ShapeDtypeStruct((n_tiles * tm * SUBLANES, LANES), F32),
        grid_spec=grid_spec,
        compiler_params=_cparams(1, VMEM_LIMIT),
        name="moe_ffn",
    )(tile_expert, tile_valid, rows, rows, h_tiles, w1, w3, w2)


def _moe_combine_kernel(pos_ref, pos_next_ref, y_hbm, route_ref, x_ref, gpost_ref, gt_ref, o_ref, ybuf, sem):
    tm = x_ref.shape[0]
    yb = _gather_tiles(y_hbm, pos_ref, pos_next_ref, ybuf, sem, TOP_K * tm)
    route = route_ref[...]
    y = route[:, 2:3] * _from_tiles(yb, tm) + route[:, 3:4] * _from_tiles(yb, tm, tm)
    o_ref[...] = x_ref[...] + gt_ref[...] * _rms(y, gpost_ref[...])


def _moe_combine(y_tiles, pos_tiles, route, xall, g_post, mod5, layer, geom):
    t, d = xall.shape
    tm, tiles_per_b, ctx_row = geom
    n_tiles = t // tm
    row = lambda w: pl.BlockSpec((tm, w), lambda i: (i, 0))
    gt_spec = pl.BlockSpec((None, None, None, 1, d),
                           lambda i: (layer, 5, _mod_row(i, tiles_per_b, ctx_row), 0, 0))
    pos_spec = lambda f: pl.BlockSpec((1, 1, TOP_K * tm), lambda i: (f(i), 0, 0), memory_space=pltpu.SMEM)
    return pl.pallas_call(
        _moe_combine_kernel,
        out_shape=jax.ShapeDtypeStruct((t, d), F32),
        grid=(n_tiles,),
        in_specs=[pos_spec(lambda i: i), pos_spec(lambda i: jnp.minimum(i + 1, n_tiles - 1)),
                  pl.BlockSpec(memory_space=pl.ANY),
                  row(LANES), row(d), pl.BlockSpec(g_post.shape, lambda i: (0, 0)), gt_spec],
        out_specs=row(d),
        scratch_shapes=[pltpu.VMEM((2, TOP_K * tm * SUBLANES, LANES), F32), pltpu.SemaphoreType.DMA((2,))],
        compiler_params=_cparams(1, VMEM_LIMIT),
        name="moe_combine",
    )(pos_tiles, pos_tiles, y_tiles, route, xall, g_post, mod5)


def _route_plan(route, tm, n_experts):
    t = route.shape[0]
    n_tiles = (TOP_K * t) // tm + n_experts
    e_flat = route[:, :TOP_K].astype(jnp.int32).T.reshape(-1)
    onehot = (e_flat[:, None] == jnp.arange(n_experts, dtype=jnp.int32)[None, :]).astype(jnp.int32)
    csum = jnp.cumsum(onehot, axis=0)
    counts = csum[-1]
    rank = jnp.sum(csum * onehot, axis=1) - 1
    group_tiles = (counts + tm - 1) // tm
    group_end = jnp.cumsum(group_tiles)
    group_start = (group_end - group_tiles) * tm
    pos = jnp.sum(onehot * group_start[None, :], axis=1) + rank
    tile_group = jnp.sum((jnp.arange(n_tiles, dtype=jnp.int32)[:, None] >= group_end[None, :]).astype(jnp.int32), axis=1)
    tile_valid = (tile_group < n_experts).astype(jnp.int32)
    tile_expert = jnp.minimum(tile_group, n_experts - 1)
    token = jnp.tile(jnp.arange(t, dtype=jnp.int32), TOP_K)
    row_token = jnp.zeros((n_tiles * tm,), jnp.int32).at[pos].set(token, unique_indices=True)
    pos_tiles = pos.reshape(TOP_K, t // tm, tm).transpose(1, 0, 2).reshape(t // tm, 1, TOP_K * tm)
    return tile_expert, tile_valid, row_token, pos_tiles


def _rope_tables(seq, ctx_len):
    rows_n = seq // GRID_W
    row = jnp.repeat(jnp.arange(rows_n), GRID_W).astype(F32)
    col = jnp.tile(jnp.arange(GRID_W), rows_n).astype(F32)
    n_freq = QK_DIM // 4
    inv = ROPE_THETA ** (-jnp.arange(n_freq, dtype=F32) / n_freq)
    ang = jnp.concatenate([row[:, None] * inv, row[:, None] * inv, col[:, None] * inv, col[:, None] * inv], axis=1)
    cos, sin = jnp.cos(ang), jnp.sin(ang)
    first = (jnp.arange(QK_DIM) % (2 * n_freq)) < n_freq
    s1 = jnp.where(first[None, :], -sin, 0.0)
    s2 = jnp.where(first[None, :], 0.0, sin)
    pad = lambda a, v: jnp.concatenate([jnp.full((ctx_len, LANES), v, F32), jnp.tile(a, (1, 2))], axis=0)
    return pad(cos, 1.0), pad(s1, 0.0), pad(s2, 0.0)


def kernel(x, c, ctx, c_ctx, w_mod, b_mod, g_pre_mix, g_post_mix, g_pre_ffn, g_post_ffn, w_in, gla_gate_w2, gla_gate_b, diff_lambda, diff_subln, gla_onorm, w_out, ffn_w1, ffn_w3, ffn_w2, router, moe_w1, moe_w3, moe_w2):
    batch, seq, d = x.shape
    ctx_len = ctx.shape[1]
    depth = w_in.shape[0]
    n_experts = router.shape[-1]
    assert seq % ctx_len == 0 and ctx_len % GLA_CHUNK == 0 and d == 2 * HEAD_W == SUBLANES * LANES
    assert w_in.shape[-1] == _OFF_GLR + 2 * GATE_RANK
    tm = ctx_len
    tiles_per_b = (ctx_len + seq) // tm
    mod_rows = -(-(batch + 1) // SUBLANES) * SUBLANES
    geom = (tm, tiles_per_b, batch)

    cc = jnp.concatenate([c, c_ctx[None, :], jnp.zeros((mod_rows - batch - 1, d), F32)], axis=0)
    mod = _modulation(cc, w_mod, b_mod)
    mod5 = mod.reshape(depth, N_MOD, mod_rows, 1, d)
    rope_tabs = _rope_tables(seq, ctx_len)
    xall = jnp.concatenate([ctx, x], axis=1).reshape(batch * (ctx_len + seq), d)
    row2 = lambda a: a.reshape(1, -1)

    for l in range(depth):
        last = l == depth - 1
        lam_init = 0.8 - 0.6 * math.exp(-0.3 * l)
        w_main = w_in[l, :, :_OFF_GLR].astype(BF16)
        w_glr = jnp.pad(w_in[l, :, _OFF_GLR:], ((0, 0), (0, LANES - 2 * GATE_RANK))).astype(BF16)
        w2bd = jnp.zeros((LANES, 2 * GLA_KW), F32)
        w2bd = w2bd.at[:GATE_RANK, :GLA_KW].set(gla_gate_w2[l, 0])
        w2bd = w2bd.at[GATE_RANK:2 * GATE_RANK, GLA_KW:].set(gla_gate_w2[l, 1])
        gate_b = gla_gate_b[l].reshape(1, 2 * GLA_KW)

        dq, dk, dv, gq, gk, gv, gg, gf, gb = _inproj(
            xall, row2(g_pre_mix[l]), mod5, l, w_main, w_glr, w2bd, gate_b, rope_tabs, geom)
        od = _diff_attn(dq, dk, dv, diff_lambda[l], row2(diff_subln[l]), lam_init, not last, geom, batch)
        og = _gla(gq, gk, gv, gf, gb, geom, batch)

        moe = l % 2 == 1
        idx = l // 2
        router_pad = None
        if moe:
            router_pad = (jnp.pad(router[idx], ((0, 0), (0, LANES - n_experts))), n_experts)
        outs = _merge(od, og, gg, xall, w_out[l].astype(BF16), row2(gla_onorm[l]), row2(g_post_mix[l]),
                      row2(g_pre_ffn[l]), mod5, l, router_pad, geom)
        if moe:
            xall, h_tiles, route = outs
            tile_expert, tile_valid, row_token, pos_tiles = _route_plan(route, tm, n_experts)
            y_tiles = _moe_ffn(h_tiles, tile_expert, tile_valid, row_token, moe_w1[idx].astype(BF16),
                               moe_w3[idx].astype(BF16), moe_w2[idx].astype(BF16), tm)
            xall = _moe_combine(y_tiles, pos_tiles, route, xall, row2(g_post_ffn[l]), mod5, l, geom)
        else:
            xall, h = outs
            xall = _dense_ffn(h, xall, ffn_w1[idx].astype(BF16), ffn_w3[idx].astype(BF16),
                              ffn_w2[idx].astype(BF16), row2(g_post_ffn[l]), mod5, l, geom)

    return xall.reshape(batch, ctx_len + seq, d)[:, ctx_len:, :]
```

```python
import functools
import math

import jax
import jax.numpy as jnp
from jax import lax
from jax.experimental import pallas as pl
from jax.experimental.pallas import tpu as pltpu

F32 = jnp.float32
BF16 = jnp.bfloat16
HIGHEST = lax.Precision.HIGHEST

N_HEADS = 4
QK_DIM = 64
V_DIM = 128
HEAD_W = N_HEADS * V_DIM
GLA_KW = N_HEADS * QK_DIM
GATE_RANK = 16
GATE_NORMALIZER = 16.0
GLA_CHUNK = 64
GRID_W = 64
ROPE_THETA = 10000.0
N_MOD = 6
TOP_K = 2
EPS = 1e-6
LOG2_E = 1.4426950408889634
LANES = 128
SUBLANES = 8
MXU_DIM = 256
VMEM_LIMIT = 56 * 1024 * 1024
DMA_ISSUE_UNROLL = 8

_OFF_DQ, _OFF_DK, _OFF_DV = 0, 512, 1024
_OFF_GQ, _OFF_GK, _OFF_GV, _OFF_GG, _OFF_GLR = 1536, 1792, 2048, 2560, 3072


def _rms(x, g):
    ms = jnp.mean(x * x, axis=-1, keepdims=True)
    return x * lax.rsqrt(ms + EPS) * g


def _silu(x):
    return x * (1.0 / (1.0 + jnp.exp(-x)))


def _split2(a):
    hi = a.astype(BF16)
    return hi, (a - hi.astype(F32)).astype(BF16)


def _cparams(n_axes, vmem=None):
    return pltpu.CompilerParams(dimension_semantics=("arbitrary",) * n_axes,
                                vmem_limit_bytes=vmem)


def _mod_row(i, tiles_per_b, ctx_row):
    return jnp.where(i % tiles_per_b == 0, ctx_row, i // tiles_per_b)


def _to_tiles(ref, val):
    rows = val.shape[0]
    for c in range(SUBLANES):
        ref[pl.ds(c, rows, stride=SUBLANES), :] = val[:, LANES * c:LANES * (c + 1)]


def _from_tiles(ref, rows, row0=0):
    return jnp.concatenate(
        [ref[pl.ds(row0 * SUBLANES + c, rows, stride=SUBLANES), :] for c in range(SUBLANES)], axis=1)


def _mod_kernel(c_ref, w_ref, b_ref, o_ref):
    o_ref[...] = jnp.dot(_silu(c_ref[...]), w_ref[...], precision=HIGHEST,
                         preferred_element_type=F32) + b_ref[...]


def _modulation(cc, w_mod, b_mod):
    n_layers, d, _ = w_mod.shape
    r = cc.shape[0]
    b4 = b_mod.reshape(n_layers, N_MOD, 1, d)
    return pl.pallas_call(
        _mod_kernel,
        out_shape=jax.ShapeDtypeStruct((n_layers, N_MOD, r, d), F32),
        grid=(n_layers, N_MOD),
        in_specs=[pl.BlockSpec((r, d), lambda l, j: (0, 0)),
                  pl.BlockSpec((None, d, d), lambda l, j: (l, 0, j)),
                  pl.BlockSpec((None, None, 1, d), lambda l, j: (l, j, 0, 0))],
        out_specs=pl.BlockSpec((None, None, r, d), lambda l, j: (l, j, 0, 0)),
        compiler_params=_cparams(2),
        name="modulation",
    )(cc, w_mod, b4)


def _inproj_kernel(x_ref, g_ref, sc_ref, sh_ref, w_ref, wglr_ref, w2_ref, gb_ref,
                   cos_ref, s1_ref, s2_ref,
                   dq_ref, dk_ref, dv_ref, gq_ref, gk_ref, gv_ref, gg_ref, gf_ref, gb_out_ref):
    h = _rms(x_ref[...], g_ref[...]) * (1.0 + sc_ref[...]) + sh_ref[...]
    hb = h.astype(BF16)
    cos, s1, s2 = cos_ref[...], s1_ref[...], s2_ref[...]

    def proj(off, n):
        return jnp.dot(hb, w_ref[:, off:off + n], preferred_element_type=F32)

    def rope(p):
        outs = []
        for j in range(N_HEADS):
            xg = p[:, LANES * j:LANES * (j + 1)]
            outs.append(xg * cos + pltpu.roll(xg, LANES - 16, 1) * s1 + pltpu.roll(xg, 16, 1) * s2)
        return jnp.concatenate(outs, axis=1)

    scale = QK_DIM ** -0.5
    dq_ref[...] = (rope(proj(_OFF_DQ, HEAD_W)) * (scale * LOG2_E)).astype(BF16)
    dk_ref[...] = rope(proj(_OFF_DK, HEAD_W)).astype(BF16)
    dv_ref[...] = proj(_OFF_DV, HEAD_W).astype(BF16)
    gq_ref[...] = proj(_OFF_GQ, GLA_KW) * scale
    gk_ref[...] = proj(_OFF_GK, GLA_KW)
    gv_ref[...] = proj(_OFF_GV, HEAD_W)
    gg_ref[...] = proj(_OFF_GG, HEAD_W)
    glr = jnp.dot(hb, wglr_ref[...], preferred_element_type=F32)
    glr_hi, glr_lo = _split2(glr)
    w2_hi, w2_lo = _split2(w2_ref[...])
    z = (jnp.dot(glr_hi, w2_hi, preferred_element_type=F32) + jnp.dot(glr_hi, w2_lo, preferred_element_type=F32)
         + jnp.dot(glr_lo, w2_hi, preferred_element_type=F32)) + gb_ref[...]
    gl = (jnp.minimum(z, 0.0) - jnp.log1p(jnp.exp(-jnp.abs(z)))) * (1.0 / GATE_NORMALIZER)
    tm = gl.shape[0]
    r_i = lax.broadcasted_iota(jnp.int32, (tm, tm), 0)
    c_i = lax.broadcasted_iota(jnp.int32, (tm, tm), 1)
    same_chunk = (r_i // GLA_CHUNK) == (c_i // GLA_CHUNK)
    lower = jnp.where(same_chunk & (r_i >= c_i), 1.0, 0.0).astype(BF16)
    upper = jnp.where(same_chunk & (c_i >= r_i), 1.0, 0.0).astype(BF16)

    def tri_sum(tri, a):
        hi, lo = _split2(a)
        return jnp.dot(tri, hi, preferred_element_type=F32) + jnp.dot(tri, lo, preferred_element_type=F32)

    gf_ref[...] = tri_sum(lower, gl[:, :GLA_KW])
    gb_out_ref[...] = tri_sum(upper, gl[:, GLA_KW:])


def _inproj(xall, g_pre, mod5, layer, w_main, w_glr, w2bd, gate_b, rope_tabs, geom):
    t, d = xall.shape
    tm, tiles_per_b, ctx_row = geom
    n_tiles = t // tm

    def mod_spec(j):
        return pl.BlockSpec((None, None, None, 1, d),
                            lambda i: (layer, j, _mod_row(i, tiles_per_b, ctx_row), 0, 0))

    row = lambda w: pl.BlockSpec((tm, w), lambda i: (i, 0))
    full = lambda a: pl.BlockSpec(a.shape, lambda i: (0,) * a.ndim)
    tab = pl.BlockSpec((tm, LANES), lambda i: (i % tiles_per_b, 0))
    widths = (HEAD_W, HEAD_W, HEAD_W, GLA_KW, GLA_KW, HEAD_W, HEAD_W, GLA_KW, GLA_KW)
    dtypes = (BF16, BF16, BF16, F32, F32, F32, F32, F32, F32)
    return pl.pallas_call(
        _inproj_kernel,
        out_shape=[jax.ShapeDtypeStruct((t, w), dt) for w, dt in zip(widths, dtypes)],
        grid=(n_tiles,),
        in_specs=[row(d), full(g_pre), mod_spec(1), mod_spec(0), full(w_main), full(w_glr),
                  full(w2bd), full(gate_b), tab, tab, tab],
        out_specs=[row(w) for w in widths],
        compiler_params=_cparams(1, VMEM_LIMIT),
        name="inproj",
    )(xall, g_pre, mod5, mod5, w_main, w_glr, w2bd, gate_b, *rope_tabs)


ATTN_HEADS_PER_STEP = 4


def _diff_attn_kernel(q_ref, k_ref, v_ref, lam_ref, g_ref, o_ref, *, lam_init, ctx_len, with_ctx):
    lv = lam_ref[...]
    lam = (jnp.exp(jnp.sum(lv[0:1] * lv[1:2], axis=-1, keepdims=True))
           - jnp.exp(jnp.sum(lv[2:3] * lv[3:4], axis=-1, keepdims=True)) + lam_init)
    lane = lax.broadcasted_iota(jnp.int32, (1, LANES), 1)

    def attend(n_keys):
        ones = jnp.ones((n_keys, V_DIM), BF16)
        for hh in range(ATTN_HEADS_PER_STEP):
            cols = slice(V_DIM * hh, V_DIM * (hh + 1))
            q = q_ref[:, cols]
            k = k_ref[0:n_keys, cols]
            v_ext = jnp.concatenate([v_ref[0:n_keys, cols], ones], axis=1)

            def one(qm):
                s = lax.dot_general(qm, k, (((1,), (1,)), ((), ())), preferred_element_type=F32)
                p = jnp.exp2(s - jnp.max(s, axis=-1, keepdims=True))
                r = jnp.dot(p.astype(BF16), v_ext, preferred_element_type=F32)
                return r[:, :V_DIM] / r[:, V_DIM:V_DIM + 1]

            o = (one(jnp.where(lane < QK_DIM, q, jnp.zeros_like(q)))
                 - lam * one(jnp.where(lane >= QK_DIM, q, jnp.zeros_like(q))))
            o_ref[:, cols] = (_rms(o, g_ref[...]) * (1.0 - lam_init)).astype(o_ref.dtype)

    is_ctx = pl.program_id(2) == 0

    @pl.when(jnp.logical_not(is_ctx))
    def _():
        attend(k_ref.shape[0])

    @pl.when(is_ctx)
    def _():
        if with_ctx:
            attend(ctx_len)
        else:
            o_ref[...] = jnp.zeros_like(o_ref)


def _diff_attn(dq, dk, dv, lam_vec, subln_g, lam_init, with_ctx, geom, batch):
    t = dq.shape[0]
    tm, tiles_per_b, _ = geom
    seg = tm * tiles_per_b
    w = ATTN_HEADS_PER_STEP * V_DIM
    dk3 = dk.reshape(batch, seg, HEAD_W)
    dv3 = dv.reshape(batch, seg, HEAD_W)
    kern = functools.partial(_diff_attn_kernel, lam_init=lam_init, ctx_len=tm, with_ctx=with_ctx)
    return pl.pallas_call(
        kern,
        out_shape=jax.ShapeDtypeStruct((t, HEAD_W), BF16),
        grid=(batch, N_HEADS // ATTN_HEADS_PER_STEP, tiles_per_b),
        in_specs=[pl.BlockSpec((tm, w), lambda b, h, i: (b * tiles_per_b + i, h)),
                  pl.BlockSpec((None, seg, w), lambda b, h, i: (b, 0, h)),
                  pl.BlockSpec((None, seg, w), lambda b, h, i: (b, 0, h)),
                  pl.BlockSpec(lam_vec.shape, lambda b, h, i: (0, 0)),
                  pl.BlockSpec(subln_g.shape, lambda b, h, i: (0, 0))],
        out_specs=pl.BlockSpec((tm, w), lambda b, h, i: (b * tiles_per_b + i, h)),
        compiler_params=_cparams(3, VMEM_LIMIT),
        name="diff_attn",
    )(dq, dk3, dv3, lam_vec, subln_g)


GLA_UNROLL = 2


def _gla_kernel(q_ref, k_ref, v_ref, gf_ref, gb_ref, o_ref, *, n_chunks, n_ctx_chunks):
    c = GLA_CHUNK
    r_i = lax.broadcasted_iota(jnp.int32, (N_HEADS * c, c), 0) % c
    c_i = lax.broadcasted_iota(jnp.int32, (N_HEADS * c, c), 1)
    lane = lax.broadcasted_iota(jnp.int32, (1, GLA_KW), 1)
    head_masks = [((lane // QK_DIM) == h).astype(F32) for h in range(N_HEADS)]
    causal = (r_i >= c_i, c_i >= r_i)
    last_row = (c - 1, 0)
    mid_row = (c // 2 - 1, c - c // 2)
    g_refs = (gf_ref, gb_ref)
    nt = (((1,), (1,)), ((), ()))
    tn = (((0,), (0,)), ((), ()))

    def head_stack(a):
        return jnp.concatenate([(a * head_masks[h]).astype(BF16) for h in range(N_HEADS)], axis=0)

    def chunk(direction, chunk_idx, st):
        rows = pl.ds(pl.multiple_of(chunk_idx * c, c), c)
        q, k, v, g = q_ref[rows, :], k_ref[rows, :], v_ref[rows, :], g_refs[direction][rows, :]
        last, mid = last_row[direction], mid_row[direction]
        g_last = g[last:last + 1, :]
        g_mid = g[mid:mid + 1, :]
        ke = (k * jnp.exp(g_mid - g)).astype(BF16)
        vb = v.astype(BF16)
        att = lax.dot_general(head_stack(q * jnp.exp(g - g_mid)), ke, nt, preferred_element_type=F32)
        att = jnp.where(causal[direction], att, 0.0).astype(BF16)
        inter = lax.dot_general(head_stack(q * jnp.exp(g)), st.astype(BF16), nt,
                                preferred_element_type=F32)
        o = jnp.concatenate(
            [jnp.dot(att[c * h:c * (h + 1), :], vb[:, V_DIM * h:V_DIM * (h + 1)], preferred_element_type=F32)
             + inter[c * h:c * (h + 1), :] for h in range(N_HEADS)], axis=1)
        o_ref[rows, :] = o_ref[rows, :] + o
        v_stack = jnp.concatenate([vb[:, V_DIM * h:V_DIM * (h + 1)] for h in range(N_HEADS)], axis=0)
        ds_t = lax.dot_general(v_stack, head_stack(k * jnp.exp(g_last - g)), tn,
                               preferred_element_type=F32)
        return st * jnp.exp(g_last) + ds_t

    o_ref[...] = jnp.zeros_like(o_ref)

    def body(i, carry):
        st_f, st_b = carry
        i_b = jnp.where(i < n_ctx_chunks, n_ctx_chunks - 1 - i, n_chunks - 1 - (i - n_ctx_chunks))
        return chunk(0, i, st_f), chunk(1, i_b, st_b)

    zero = jnp.zeros((V_DIM, GLA_KW), F32)
    lax.fori_loop(0, n_chunks, body, (zero, zero), unroll=GLA_UNROLL)


def _gla(gq, gk, gv, gf, gb, geom, batch):
    tm, tiles_per_b, _ = geom
    seg = tm * tiles_per_b
    r3 = lambda a: a.reshape(batch, seg, a.shape[-1])
    spec = lambda w: pl.BlockSpec((None, seg, w), lambda b: (b, 0, 0))
    kern = functools.partial(_gla_kernel, n_chunks=seg // GLA_CHUNK, n_ctx_chunks=tm // GLA_CHUNK)
    out = pl.pallas_call(
        kern,
        out_shape=jax.ShapeDtypeStruct((batch, seg, HEAD_W), F32),
        grid=(batch,),
        in_specs=[spec(GLA_KW), spec(GLA_KW), spec(HEAD_W), spec(GLA_KW), spec(GLA_KW)],
        out_specs=spec(HEAD_W),
        compiler_params=_cparams(1, VMEM_LIMIT),
        name="gla",
    )(r3(gq), r3(gk), r3(gv), r3(gf), r3(gb))
    return out.reshape(batch * seg, HEAD_W)


def _merge_kernel(od_ref, og_ref, gg_ref, x_ref, wout_ref, onorm_ref, gpost_ref, gt_ref,
                  gpre_ref, sc_ref, sh_ref, *rest, n_experts):
    if n_experts:
        router_ref, xo_ref, h_ref, route_ref = rest
    else:
        xo_ref, h_ref = rest
    og = og_ref[...]
    onorm = onorm_ref[...]
    ogn = jnp.concatenate([_rms(og[:, V_DIM * h:V_DIM * (h + 1)], onorm) for h in range(N_HEADS)], axis=1)
    ogn = (ogn * _silu(gg_ref[...])).astype(BF16)
    y = (jnp.dot(od_ref[...], wout_ref[0:HEAD_W, :], preferred_element_type=F32)
         + jnp.dot(ogn, wout_ref[HEAD_W:2 * HEAD_W, :], preferred_element_type=F32))
    x_new = x_ref[...] + gt_ref[...] * _rms(y, gpost_ref[...])
    xo_ref[...] = x_new
    h = _rms(x_new, gpre_ref[...]) * (1.0 + sc_ref[...]) + sh_ref[...]
    if not n_experts:
        h_ref[...] = h.astype(h_ref.dtype)
        return
    _to_tiles(h_ref, h)
    h_hi, h_lo = _split2(h)
    r_hi, r_lo = _split2(router_ref[...])
    logits = (jnp.dot(h_hi, r_hi, preferred_element_type=F32) + jnp.dot(h_hi, r_lo, preferred_element_type=F32)
              + jnp.dot(h_lo, r_hi, preferred_element_type=F32))
    lane = lax.broadcasted_iota(jnp.int32, logits.shape, 1)
    neg = jnp.float32(-jnp.inf)
    lg = jnp.where(lane < n_experts, logits, neg)
    m1 = jnp.max(lg, axis=-1, keepdims=True)
    i1 = jnp.min(jnp.where(lg == m1, lane, LANES), axis=-1, keepdims=True)
    lg2 = jnp.where(lane == i1, neg, lg)
    m2 = jnp.max(lg2, axis=-1, keepdims=True)
    i2 = jnp.min(jnp.where(lg2 == m2, lane, LANES), axis=-1, keepdims=True)
    e2 = jnp.exp(m2 - m1)
    w1 = 1.0 / (1.0 + e2)
    w2 = e2 / (1.0 + e2)
    route_ref[...] = jnp.where(lane == 0, i1.astype(F32),
                               jnp.where(lane == 1, i2.astype(F32),
                                         jnp.where(lane == 2, w1, jnp.where(lane == 3, w2, 0.0))))


def _merge(od, og, gg, xall, w_out, onorm, g_post, g_pre_ffn, mod5, layer, router_pad, geom):
    t, d = xall.shape
    tm, tiles_per_b, ctx_row = geom
    n_experts = 0 if router_pad is None else router_pad[1]

    def mod_spec(j):
        return pl.BlockSpec((None, None, None, 1, d),
                            lambda i: (layer, j, _mod_row(i, tiles_per_b, ctx_row), 0, 0))

    row = lambda w: pl.BlockSpec((tm, w), lambda i: (i, 0))
    full = lambda a: pl.BlockSpec(a.shape, lambda i: (0,) * a.ndim)
    in_specs = [row(HEAD_W), row(HEAD_W), row(HEAD_W), row(d), full(w_out), full(onorm), full(g_post),
                mod_spec(2), full(g_pre_ffn), mod_spec(4), mod_spec(3)]
    args = [od, og, gg, xall, w_out, onorm, g_post, mod5, g_pre_ffn, mod5, mod5]
    out_shape = [jax.ShapeDtypeStruct((t, d), F32)]
    out_specs = [row(d)]
    if n_experts:
        in_specs.append(full(router_pad[0]))
        args.append(router_pad[0])
        out_shape += [jax.ShapeDtypeStruct((t * SUBLANES, LANES), F32), jax.ShapeDtypeStruct((t, LANES), F32)]
        out_specs += [pl.BlockSpec((tm * SUBLANES, LANES), lambda i: (i, 0)), row(LANES)]
    else:
        out_shape.append(jax.ShapeDtypeStruct((t, d), BF16))
        out_specs.append(row(d))
    return pl.pallas_call(
        functools.partial(_merge_kernel, n_experts=n_experts),
        out_shape=out_shape,
        grid=(t // tm,),
        in_specs=in_specs,
        out_specs=out_specs,
        compiler_params=_cparams(1, VMEM_LIMIT),
        name="merge_outproj",
    )(*args)


def _swiglu(xb, w1_ref, w3_ref, w2_ref):
    d_ff = w1_ref.shape[-1]
    cut = -(-(d_ff // 2) // MXU_DIM) * MXU_DIM
    y = None
    for c0, c1 in ((0, cut), (cut, d_ff)):
        a = jnp.dot(xb, w1_ref[:, c0:c1], preferred_element_type=F32)
        b = jnp.dot(xb, w3_ref[:, c0:c1], preferred_element_type=F32)
        m = (_silu(a) * b).astype(BF16)
        part = jnp.dot(m, w2_ref[c0:c1, :], preferred_element_type=F32)
        y = part if y is None else y + part
    return y


def _dense_ffn_kernel(h_ref, x_ref, w1_ref, w3_ref, w2_ref, gpost_ref, gt_ref, o_ref):
    y = _swiglu(h_ref[...], w1_ref, w3_ref, w2_ref)
    o_ref[...] = x_ref[...] + gt_ref[...] * _rms(y, gpost_ref[...])


def _dense_ffn(h, xall, w1, w3, w2, g_post, mod5, layer, geom):
    t, d = xall.shape
    tm, tiles_per_b, ctx_row = geom
    row = lambda w: pl.BlockSpec((tm, w), lambda i: (i, 0))
    resident = lambda a: pl.BlockSpec(a.shape, lambda i: (0,) * a.ndim, pipeline_mode=pl.Buffered(1))
    gt_spec = pl.BlockSpec((None, None, None, 1, d),
                           lambda i: (layer, 5, _mod_row(i, tiles_per_b, ctx_row), 0, 0))
    return pl.pallas_call(
        _dense_ffn_kernel,
        out_shape=jax.ShapeDtypeStruct((t, d), F32),
        grid=(t // tm,),
        in_specs=[row(d), row(d), resident(w1), resident(w3), resident(w2),
                  pl.BlockSpec(g_post.shape, lambda i: (0, 0)), gt_spec],
        out_specs=row(d),
        compiler_params=_cparams(1, VMEM_LIMIT),
        name="dense_ffn",
    )(h, xall, w1, w3, w2, g_post, mod5)


def _tile_copy(src, src_tok, dst, dst_tok, sem):
    s0 = pl.multiple_of(src_tok * SUBLANES, SUBLANES)
    d0 = pl.multiple_of(dst_tok * SUBLANES, SUBLANES)
    return pltpu.make_async_copy(src.at[pl.ds(s0, SUBLANES), :], dst.at[pl.ds(d0, SUBLANES), :], sem)


def _gather_tiles(src_hbm, idx_ref, idx_next_ref, buf, sem, n_copies):
    i = pl.program_id(0)
    slot = lax.rem(i, 2)

    def issue(ref, s):
        def start(r, carry):
            _tile_copy(src_hbm, ref[0, 0, r], buf.at[s], r, sem.at[s]).start()
            return carry
        lax.fori_loop(0, n_copies, start, 0, unroll=DMA_ISSUE_UNROLL)

    @pl.when(i == 0)
    def _():
        issue(idx_ref, 0)

    @pl.when(i + 1 < pl.num_programs(0))
    def _():
        issue(idx_next_ref, 1 - slot)

    def wait(r, carry):
        _tile_copy(src_hbm, 0, buf.at[slot], r, sem.at[slot]).wait()
        return carry

    lax.fori_loop(0, n_copies, wait, 0, unroll=DMA_ISSUE_UNROLL)
    return buf.at[slot]


def _moe_ffn_kernel(te_ref, tv_ref, rows_ref, rows_next_ref, h_hbm, w1_ref, w3_ref, w2_ref, y_ref,
                    xbuf, sem, *, tm):
    del te_ref
    xb = _gather_tiles(h_hbm, rows_ref, rows_next_ref, xbuf, sem, tm)
    valid = tv_ref[pl.program_id(0)] == 1

    @pl.when(valid)
    def _():
        _to_tiles(y_ref, _swiglu(_from_tiles(xb, tm).astype(BF16), w1_ref, w3_ref, w2_ref))

    @pl.when(jnp.logical_not(valid))
    def _():
        y_ref[...] = jnp.zeros_like(y_ref)


def _moe_ffn(h_tiles, tile_expert, tile_valid, row_token, w1, w3, w2, tm):
    n_tiles = tile_expert.shape[0]
    _, d, d_ff = w1.shape
    rows = row_token.reshape(n_tiles, 1, tm)
    idx_spec = lambda f: pl.BlockSpec((1, 1, tm), lambda i, te, tv: (f(i), 0, 0), memory_space=pltpu.SMEM)
    grid_spec = pltpu.PrefetchScalarGridSpec(
        num_scalar_prefetch=2,
        grid=(n_tiles,),
        in_specs=[idx_spec(lambda i: i), idx_spec(lambda i: jnp.minimum(i + 1, n_tiles - 1)),
                  pl.BlockSpec(memory_space=pl.ANY),
                  pl.BlockSpec((None, d, d_ff), lambda i, te, tv: (te[i], 0, 0)),
                  pl.BlockSpec((None, d, d_ff), lambda i, te, tv: (te[i], 0, 0)),
                  pl.BlockSpec((None, d_ff, d), lambda i, te, tv: (te[i], 0, 0))],
        out_specs=pl.BlockSpec((tm * SUBLANES, LANES), lambda i, te, tv: (i, 0)),
        scratch_shapes=[pltpu.VMEM((2, tm * SUBLANES, LANES), F32), pltpu.SemaphoreType.DMA((2,))],
    )
    return pl.pallas_call(
        functools.partial(_moe_ffn_kernel, tm=tm),
        out_shape=jax.ShapeDtypeStruct((n_tiles * tm * SUBLANES, LANES), F32),
        grid_spec=grid_spec,
        compiler_params=_cparams(1, VMEM_LIMIT),
        name="moe_ffn",
    )(tile_expert, tile_valid, rows, rows, h_tiles, w1, w3, w2)


def _moe_combine_kernel(pos_ref, pos_next_ref, y_hbm, route_ref, x_ref, gpost_ref, gt_ref, o_ref, ybuf, sem):
    tm = x_ref.shape[0]
    yb = _gather_tiles(y_hbm, pos_ref, pos_next_ref, ybuf, sem, TOP_K * tm)
    route = route_ref[...]
    y = route[:, 2:3] * _from_tiles(yb, tm) + route[:, 3:4] * _from_tiles(yb, tm, tm)
    o_ref[...] = x_ref[...] + gt_ref[...] * _rms(y, gpost_ref[...])


def _moe_combine(y_tiles, pos_tiles, route, xall, g_post, mod5, layer, geom):
    t, d = xall.shape
    tm, tiles_per_b, ctx_row = geom
    n_tiles = t // tm
    row = lambda w: pl.BlockSpec((tm, w), lambda i: (i, 0))
    gt_spec = pl.BlockSpec((None, None, None, 1, d),
                           lambda i: (layer, 5, _mod_row(i, tiles_per_b, ctx_row), 0, 0))
    pos_spec = lambda f: pl.BlockSpec((1, 1, TOP_K * tm), lambda i: (f(i), 0, 0), memory_space=pltpu.SMEM)
    return pl.pallas_call(
        _moe_combine_kernel,
        out_shape=jax.ShapeDtypeStruct((t, d), F32),
        grid=(n_tiles,),
        in_specs=[pos_spec(lambda i: i), pos_spec(lambda i: jnp.minimum(i + 1, n_tiles - 1)),
                  pl.BlockSpec(memory_space=pl.ANY),
                  row(LANES), row(d), pl.BlockSpec(g_post.shape, lambda i: (0, 0)), gt_spec],
        out_specs=row(d),
        scratch_shapes=[pltpu.VMEM((2, TOP_K * tm * SUBLANES, LANES), F32), pltpu.SemaphoreType.DMA((2,))],
        compiler_params=_cparams(1, VMEM_LIMIT),
        name="moe_combine",
    )(pos_tiles, pos_tiles, y_tiles, route, xall, g_post, mod5)


def _route_plan(route, tm, n_experts):
    t = route.shape[0]
    n_tiles = (TOP_K * t) // tm + n_experts
    e_flat = route[:, :TOP_K].astype(jnp.int32).T.reshape(-1)
    onehot = (e_flat[:, None] == jnp.arange(n_experts, dtype=jnp.int32)[None, :]).astype(jnp.int32)
    csum = jnp.cumsum(onehot, axis=0)
    counts = csum[-1]
    rank = jnp.sum(csum * onehot, axis=1) - 1
    group_tiles = (counts + tm - 1) // tm
    group_end = jnp.cumsum(group_tiles)
    group_start = (group_end - group_tiles) * tm
    pos = jnp.sum(onehot * group_start[None, :], axis=1) + rank
    tile_group = jnp.sum((jnp.arange(n_tiles, dtype=jnp.int32)[:, None] >= group_end[None, :]).astype(jnp.int32), axis=1)
    tile_valid = (tile_group < n_experts).astype(jnp.int32)
    tile_expert = jnp.minimum(tile_group, n_experts - 1)
    token = jnp.tile(jnp.arange(t, dtype=jnp.int32), TOP_K)
    row_token = jnp.zeros((n_tiles * tm,), jnp.int32).at[pos].set(token, unique_indices=True)
    pos_tiles = pos.reshape(TOP_K, t // tm, tm).transpose(1, 0, 2).reshape(t // tm, 1, TOP_K * tm)
    return tile_expert, tile_valid, row_token, pos_tiles


def _rope_tables(seq, ctx_len):
    rows_n = seq // GRID_W
    row = jnp.repeat(jnp.arange(rows_n), GRID_W).astype(F32)
    col = jnp.tile(jnp.arange(GRID_W), rows_n).astype(F32)
    n_freq = QK_DIM // 4
    inv = ROPE_THETA ** (-jnp.arange(n_freq, dtype=F32) / n_freq)
    ang = jnp.concatenate([row[:, None] * inv, row[:, None] * inv, col[:, None] * inv, col[:, None] * inv], axis=1)
    cos, sin = jnp.cos(ang), jnp.sin(ang)
    first = (jnp.arange(QK_DIM) % (2 * n_freq)) < n_freq
    s1 = jnp.where(first[None, :], -sin, 0.0)
    s2 = jnp.where(first[None, :], 0.0, sin)
    pad = lambda a, v: jnp.concatenate([jnp.full((ctx_len, LANES), v, F32), jnp.tile(a, (1, 2))], axis=0)
    return pad(cos, 1.0), pad(s1, 0.0), pad(s2, 0.0)


def kernel(x, c, ctx, c_ctx, w_mod, b_mod, g_pre_mix, g_post_mix, g_pre_ffn, g_post_ffn, w_in, gla_gate_w2, gla_gate_b, diff_lambda, diff_subln, gla_onorm, w_out, ffn_w1, ffn_w3, ffn_w2, router, moe_w1, moe_w3, moe_w2):
    batch, seq, d = x.shape
    ctx_len = ctx.shape[1]
    depth = w_in.shape[0]
    n_experts = router.shape[-1]
    assert seq % ctx_len == 0 and ctx_len % GLA_CHUNK == 0 and d == 2 * HEAD_W == SUBLANES * LANES
    assert w_in.shape[-1] == _OFF_GLR + 2 * GATE_RANK
    tm = ctx_len
    tiles_per_b = (ctx_len + seq) // tm
    mod_rows = -(-(batch + 1) // SUBLANES) * SUBLANES
    geom = (tm, tiles_per_b, batch)

    cc = jnp.concatenate([c, c_ctx[None, :], jnp.zeros((mod_rows - batch - 1, d), F32)], axis=0)
    mod = _modulation(cc, w_mod, b_mod)
    mod5 = mod.reshape(depth, N_MOD, mod_rows, 1, d)
    rope_tabs = _rope_tables(seq, ctx_len)
    xall = jnp.concatenate([ctx, x], axis=1).reshape(batch * (ctx_len + seq), d)
    row2 = lambda a: a.reshape(1, -1)

    for l in range(depth):
        last = l == depth - 1
        lam_init = 0.8 - 0.6 * math.exp(-0.3 * l)
        w_main = w_in[l, :, :_OFF_GLR].astype(BF16)
        w_glr = jnp.pad(w_in[l, :, _OFF_GLR:], ((0, 0), (0, LANES - 2 * GATE_RANK))).astype(BF16)
        w2bd = jnp.zeros((LANES, 2 * GLA_KW), F32)
        w2bd = w2bd.at[:GATE_RANK, :GLA_KW].set(gla_gate_w2[l, 0])
        w2bd = w2bd.at[GATE_RANK:2 * GATE_RANK, GLA_KW:].set(gla_gate_w2[l, 1])
        gate_b = gla_gate_b[l].reshape(1, 2 * GLA_KW)

        dq, dk, dv, gq, gk, gv, gg, gf, gb = _inproj(
            xall, row2(g_pre_mix[l]), mod5, l, w_main, w_glr, w2bd, gate_b, rope_tabs, geom)
        od = _diff_attn(dq, dk, dv, diff_lambda[l], row2(diff_subln[l]), lam_init, not last, geom, batch)
        og = _gla(gq, gk, gv, gf, gb, geom, batch)

        moe = l % 2 == 1
        idx = l // 2
        router_pad = None
        if moe:
            router_pad = (jnp.pad(router[idx], ((0, 0), (0, LANES - n_experts))), n_experts)
        outs = _merge(od, og, gg, xall, w_out[l].astype(BF16), row2(gla_onorm[l]), row2(g_post_mix[l]),
                      row2(g_pre_ffn[l]), mod5, l, router_pad, geom)
        if moe:
            xall, h_tiles, route = outs
            tile_expert, tile_valid, row_token, pos_tiles = _route_plan(route, tm, n_experts)
            y_tiles = _moe_ffn(h_tiles, tile_expert, tile_valid, row_token, moe_w1[idx].astype(BF16),
                               moe_w3[idx].astype(BF16), moe_w2[idx].astype(BF16), tm)
            xall = _moe_combine(y_tiles, pos_tiles, route, xall, row2(g_post_ffn[l]), mod5, l, geom)
        else:
            xall, h = outs
            xall = _dense_ffn(h, xall, ffn_w1[idx].astype(BF16), ffn_w3[idx].astype(BF16),
                              ffn_w2[idx].astype(BF16), row2(g_post_ffn[l]), mod5, l, geom)

    return xall.reshape(batch, ctx_len + seq, d)[:, ctx_len:, :]
```

```python
import functools
import math

import jax
import jax.numpy as jnp
from jax import lax
from jax.experimental import pallas as pl
from jax.experimental.pallas import tpu as pltpu

F32 = jnp.float32
BF16 = jnp.bfloat16
HIGHEST = lax.Precision.HIGHEST

N_HEADS = 4
QK_DIM = 64
V_DIM = 128
HEAD_W = N_HEADS * V_DIM
GLA_KW = N_HEADS * QK_DIM
GATE_RANK = 16
GATE_NORMALIZER = 16.0
GLA_CHUNK = 64
GRID_W = 64
ROPE_THETA = 10000.0
N_MOD = 6
TOP_K = 2
EPS = 1e-6
LOG2_E = 1.4426950408889634
LANES = 128
SUBLANES = 8
MXU_DIM = 256
VMEM_LIMIT = 56 * 1024 * 1024
DMA_ISSUE_UNROLL = 8

_OFF_DQ, _OFF_DK, _OFF_DV = 0, 512, 1024
_OFF_GQ, _OFF_GK, _OFF_GV, _OFF_GG, _OFF_GLR = 1536, 1792, 2048, 2560, 3072


def _rms(x, g):
    ms = jnp.mean(x * x, axis=-1, keepdims=True)
    return x * lax.rsqrt(ms + EPS) * g


def _silu(x):
    return x * (1.0 / (1.0 + jnp.exp(-x)))


def _split2(a):
    hi = a.astype(BF16)
    return hi, (a - hi.astype(F32)).astype(BF16)


def _cparams(n_axes, vmem=None):
    return pltpu.CompilerParams(dimension_semantics=("arbitrary",) * n_axes,
                                vmem_limit_bytes=vmem)


def _mod_row(i, tiles_per_b, ctx_row):
    return jnp.where(i % tiles_per_b == 0, ctx_row, i // tiles_per_b)


def _to_tiles(ref, val):
    rows = val.shape[0]
    for c in range(SUBLANES):
        ref[pl.ds(c, rows, stride=SUBLANES), :] = val[:, LANES * c:LANES * (c + 1)]


def _from_tiles(ref, rows, row0=0):
    return jnp.concatenate(
        [ref[pl.ds(row0 * SUBLANES + c, rows, stride=SUBLANES), :] for c in range(SUBLANES)], axis=1)


def _mod_kernel(c_ref, w_ref, b_ref, o_ref):
    o_ref[...] = jnp.dot(_silu(c_ref[...]), w_ref[...], precision=HIGHEST,
                         preferred_element_type=F32) + b_ref[...]


def _modulation(cc, w_mod, b_mod):
    n_layers, d, _ = w_mod.shape
    r = cc.shape[0]
    b4 = b_mod.reshape(n_layers, N_MOD, 1, d)
    return pl.pallas_call(
        _mod_kernel,
        out_shape=jax.ShapeDtypeStruct((n_layers, N_MOD, r, d), F32),
        grid=(n_layers, N_MOD),
        in_specs=[pl.BlockSpec((r, d), lambda l, j: (0, 0)),
                  pl.BlockSpec((None, d, d), lambda l, j: (l, 0, j)),
                  pl.BlockSpec((None, None, 1, d), lambda l, j: (l, j, 0, 0))],
        out_specs=pl.BlockSpec((None, None, r, d), lambda l, j: (l, j, 0, 0)),
        compiler_params=_cparams(2),
        name="modulation",
    )(cc, w_mod, b4)


def _inproj_kernel(x_ref, g_ref, sc_ref, sh_ref, w_ref, wglr_ref, w2_ref, gb_ref, tri_ref,
                   cos_ref, s1_ref, s2_ref,
                   dq_ref, dk_ref, dv_ref, gq_ref, gk_ref, gv_ref, gg_ref, gf_ref, gb_out_ref):
    h = _rms(x_ref[...], g_ref[...]) * (1.0 + sc_ref[...]) + sh_ref[...]
    hb = h.astype(BF16)
    cos, s1, s2 = cos_ref[...], s1_ref[...], s2_ref[...]

    def proj(off, n):
        return jnp.dot(hb, w_ref[:, off:off + n], preferred_element_type=F32)

    def rope(p):
        outs = []
        for j in range(N_HEADS):
            xg = p[:, LANES * j:LANES * (j + 1)]
            outs.append(xg * cos + pltpu.roll(xg, LANES - 16, 1) * s1 + pltpu.roll(xg, 16, 1) * s2)
        return jnp.concatenate(outs, axis=1)

    scale = QK_DIM ** -0.5
    dq_ref[...] = (rope(proj(_OFF_DQ, HEAD_W)) * (scale * LOG2_E)).astype(BF16)
    dk_ref[...] = rope(proj(_OFF_DK, HEAD_W)).astype(BF16)
    dv_ref[...] = proj(_OFF_DV, HEAD_W).astype(BF16)
    gq_ref[...] = proj(_OFF_GQ, GLA_KW) * scale
    gk_ref[...] = proj(_OFF_GK, GLA_KW)
    gv_ref[...] = proj(_OFF_GV, HEAD_W)
    gg_ref[...] = proj(_OFF_GG, HEAD_W)
    glr = jnp.dot(hb, wglr_ref[...], preferred_element_type=F32)
    glr_hi, glr_lo = _split2(glr)
    w2_hi, w2_lo = w2_ref[0], w2_ref[1]
    z = (jnp.dot(glr_hi, w2_hi, preferred_element_type=F32) + jnp.dot(glr_hi, w2_lo, preferred_element_type=F32)
         + jnp.dot(glr_lo, w2_hi, preferred_element_type=F32)) + gb_ref[...]
    gl = (jnp.minimum(z, 0.0) - jnp.log1p(jnp.exp(-jnp.abs(z)))) * (1.0 / GATE_NORMALIZER)

    def tri_sum(tri, a):
        hi, lo = _split2(a)
        return jnp.dot(tri, hi, preferred_element_type=F32) + jnp.dot(tri, lo, preferred_element_type=F32)

    gf_ref[...] = tri_sum(tri_ref[0], gl[:, :GLA_KW])
    gb_out_ref[...] = tri_sum(tri_ref[1], gl[:, GLA_KW:])


def _chunk_tri(tm):
    r = jnp.arange(tm)[:, None]
    c = jnp.arange(tm)[None, :]
    same = (r // GLA_CHUNK) == (c // GLA_CHUNK)
    return jnp.stack([same & (r >= c), same & (c >= r)]).astype(BF16)


def _inproj(xall, g_pre, mod5, layer, w_main, w_glr, w2bd, gate_b, rope_tabs, geom):
    t, d = xall.shape
    tm, tiles_per_b, ctx_row = geom
    n_tiles = t // tm
    tri = _chunk_tri(tm)

    def mod_spec(j):
        return pl.BlockSpec((None, None, None, 1, d),
                            lambda i: (layer, j, _mod_row(i, tiles_per_b, ctx_row), 0, 0))

    row = lambda w: pl.BlockSpec((tm, w), lambda i: (i, 0))
    full = lambda a: pl.BlockSpec(a.shape, lambda i: (0,) * a.ndim)
    tab = pl.BlockSpec((tm, LANES), lambda i: (i % tiles_per_b, 0))
    widths = (HEAD_W, HEAD_W, HEAD_W, GLA_KW, GLA_KW, HEAD_W, HEAD_W, GLA_KW, GLA_KW)
    dtypes = (BF16, BF16, BF16, F32, F32, F32, F32, F32, F32)
    return pl.pallas_call(
        _inproj_kernel,
        out_shape=[jax.ShapeDtypeStruct((t, w), dt) for w, dt in zip(widths, dtypes)],
        grid=(n_tiles,),
        in_specs=[row(d), full(g_pre), mod_spec(1), mod_spec(0), full(w_main), full(w_glr),
                  full(w2bd), full(gate_b), full(tri), tab, tab, tab],
        out_specs=[row(w) for w in widths],
        compiler_params=_cparams(1, VMEM_LIMIT),
        name="inproj",
    )(xall, g_pre, mod5, mod5, w_main, w_glr, w2bd, gate_b, tri, *rope_tabs)


ATTN_HEADS_PER_STEP = 4


def _diff_attn_kernel(q_ref, k_ref, v_ref, lam_ref, g_ref, o_ref, *, lam_init, ctx_len, with_ctx):
    lv = lam_ref[...]
    lam = (jnp.exp(jnp.sum(lv[0:1] * lv[1:2], axis=-1, keepdims=True))
           - jnp.exp(jnp.sum(lv[2:3] * lv[3:4], axis=-1, keepdims=True)) + lam_init)
    lane = lax.broadcasted_iota(jnp.int32, (1, LANES), 1)

    def attend(n_keys):
        ones = jnp.ones((n_keys, V_DIM), BF16)
        for hh in range(ATTN_HEADS_PER_STEP):
            cols = slice(V_DIM * hh, V_DIM * (hh + 1))
            q = q_ref[:, cols]
            k = k_ref[0:n_keys, cols]
            v_ext = jnp.concatenate([v_ref[0:n_keys, cols], ones], axis=1)

            def one(qm):
                s = lax.dot_general(qm, k, (((1,), (1,)), ((), ())), preferred_element_type=F32)
                p = jnp.exp2(s - jnp.max(s, axis=-1, keepdims=True))
                r = jnp.dot(p.astype(BF16), v_ext, preferred_element_type=F32)
                return r[:, :V_DIM] / r[:, V_DIM:V_DIM + 1]

            o = (one(jnp.where(lane < QK_DIM, q, jnp.zeros_like(q)))
                 - lam * one(jnp.where(lane >= QK_DIM, q, jnp.zeros_like(q))))
            o_ref[:, cols] = (_rms(o, g_ref[...]) * (1.0 - lam_init)).astype(o_ref.dtype)

    is_ctx = pl.program_id(2) == 0

    @pl.when(jnp.logical_not(is_ctx))
    def _():
        attend(k_ref.shape[0])

    @pl.when(is_ctx)
    def _():
        if with_ctx:
            attend(ctx_len)
        else:
            o_ref[...] = jnp.zeros_like(o_ref)


def _diff_attn(dq, dk, dv, lam_vec, subln_g, lam_init, with_ctx, geom, batch):
    t = dq.shape[0]
    tm, tiles_per_b, _ = geom
    seg = tm * tiles_per_b
    w = ATTN_HEADS_PER_STEP * V_DIM
    dk3 = dk.reshape(batch, seg, HEAD_W)
    dv3 = dv.reshape(batch, seg, HEAD_W)
    kern = functools.partial(_diff_attn_kernel, lam_init=lam_init, ctx_len=tm, with_ctx=with_ctx)
    return pl.pallas_call(
        kern,
        out_shape=jax.ShapeDtypeStruct((t, HEAD_W), BF16),
        grid=(batch, N_HEADS // ATTN_HEADS_PER_STEP, tiles_per_b),
        in_specs=[pl.BlockSpec((tm, w), lambda b, h, i: (b * tiles_per_b + i, h)),
                  pl.BlockSpec((None, seg, w), lambda b, h, i: (b, 0, h)),
                  pl.BlockSpec((None, seg, w), lambda b, h, i: (b, 0, h)),
                  pl.BlockSpec(lam_vec.shape, lambda b, h, i: (0, 0)),
                  pl.BlockSpec(subln_g.shape, lambda b, h, i: (0, 0))],
        out_specs=pl.BlockSpec((tm, w), lambda b, h, i: (b * tiles_per_b + i, h)),
        compiler_params=_cparams(3, VMEM_LIMIT),
        name="diff_attn",
    )(dq, dk3, dv3, lam_vec, subln_g)


GLA_UNROLL = 4


def _gla_kernel(q_ref, k_ref, v_ref, gf_ref, gb_ref, o_ref, *, n_chunks, n_ctx_chunks):
    c = GLA_CHUNK
    r_i = lax.broadcasted_iota(jnp.int32, (N_HEADS * c, c), 0) % c
    c_i = lax.broadcasted_iota(jnp.int32, (N_HEADS * c, c), 1)
    lane = lax.broadcasted_iota(jnp.int32, (1, GLA_KW), 1)
    head_masks = [((lane // QK_DIM) == h).astype(F32) for h in range(N_HEADS)]
    causal = (r_i >= c_i, c_i >= r_i)
    last_row = (c - 1, 0)
    mid_row = (c // 2 - 1, c - c // 2)
    g_refs = (gf_ref, gb_ref)
    nt = (((1,), (1,)), ((), ()))
    tn = (((0,), (0,)), ((), ()))

    def head_stack(a):
        return jnp.concatenate([(a * head_masks[h]).astype(BF16) for h in range(N_HEADS)], axis=0)

    def chunk(direction, chunk_idx, st):
        rows = pl.ds(pl.multiple_of(chunk_idx * c, c), c)
        q, k, v, g = q_ref[rows, :], k_ref[rows, :], v_ref[rows, :], g_refs[direction][rows, :]
        last, mid = last_row[direction], mid_row[direction]
        g_last = g[last:last + 1, :]
        g_mid = g[mid:mid + 1, :]
        ke = (k * jnp.exp(g_mid - g)).astype(BF16)
        vb = v.astype(BF16)
        att = lax.dot_general(head_stack(q * jnp.exp(g - g_mid)), ke, nt, preferred_element_type=F32)
        att = jnp.where(causal[direction], att, 0.0).astype(BF16)
        inter = lax.dot_general(head_stack(q * jnp.exp(g)), st.astype(BF16), nt,
                                preferred_element_type=F32)
        o = jnp.concatenate(
            [jnp.dot(att[c * h:c * (h + 1), :], vb[:, V_DIM * h:V_DIM * (h + 1)], preferred_element_type=F32)
             + inter[c * h:c * (h + 1), :] for h in range(N_HEADS)], axis=1)
        o_ref[rows, :] = o_ref[rows, :] + o
        v_stack = jnp.concatenate([vb[:, V_DIM * h:V_DIM * (h + 1)] for h in range(N_HEADS)], axis=0)
        ds_t = lax.dot_general(v_stack, head_stack(k * jnp.exp(g_last - g)), tn,
                               preferred_element_type=F32)
        return st * jnp.exp(g_last) + ds_t

    o_ref[...] = jnp.zeros_like(o_ref)

    def body(i, carry):
        st_f, st_b = carry
        i_b = jnp.where(i < n_ctx_chunks, n_ctx_chunks - 1 - i, n_chunks - 1 - (i - n_ctx_chunks))
        return chunk(0, i, st_f), chunk(1, i_b, st_b)

    zero = jnp.zeros((V_DIM, GLA_KW), F32)
    lax.fori_loop(0, n_chunks, body, (zero, zero), unroll=GLA_UNROLL)


def _gla(gq, gk, gv, gf, gb, geom, batch):
    tm, tiles_per_b, _ = geom
    seg = tm * tiles_per_b
    r3 = lambda a: a.reshape(batch, seg, a.shape[-1])
    spec = lambda w: pl.BlockSpec((None, seg, w), lambda b: (b, 0, 0))
    kern = functools.partial(_gla_kernel, n_chunks=seg // GLA_CHUNK, n_ctx_chunks=tm // GLA_CHUNK)
    out = pl.pallas_call(
        kern,
        out_shape=jax.ShapeDtypeStruct((batch, seg, HEAD_W), F32),
        grid=(batch,),
        in_specs=[spec(GLA_KW), spec(GLA_KW), spec(HEAD_W), spec(GLA_KW), spec(GLA_KW)],
        out_specs=spec(HEAD_W),
        compiler_params=_cparams(1, VMEM_LIMIT),
        name="gla",
    )(r3(gq), r3(gk), r3(gv), r3(gf), r3(gb))
    return out.reshape(batch * seg, HEAD_W)


def _merge_kernel(od_ref, og_ref, gg_ref, x_ref, wout_ref, onorm_ref, gpost_ref, gt_ref,
                  gpre_ref, sc_ref, sh_ref, *rest, n_experts):
    if n_experts:
        router_ref, xo_ref, h_ref, route_ref = rest
    else:
        xo_ref, h_ref = rest
    og = og_ref[...]
    onorm = onorm_ref[...]
    ogn = jnp.concatenate([_rms(og[:, V_DIM * h:V_DIM * (h + 1)], onorm) for h in range(N_HEADS)], axis=1)
    ogn = (ogn * _silu(gg_ref[...])).astype(BF16)
    y = (jnp.dot(od_ref[...], wout_ref[0:HEAD_W, :], preferred_element_type=F32)
         + jnp.dot(ogn, wout_ref[HEAD_W:2 * HEAD_W, :], preferred_element_type=F32))
    x_new = x_ref[...] + gt_ref[...] * _rms(y, gpost_ref[...])
    xo_ref[...] = x_new
    h = _rms(x_new, gpre_ref[...]) * (1.0 + sc_ref[...]) + sh_ref[...]
    if not n_experts:
        h_ref[...] = h.astype(h_ref.dtype)
        return
    _to_tiles(h_ref, h)
    h_hi, h_lo = _split2(h)
    r_hi, r_lo = _split2(router_ref[...])
    logits = (jnp.dot(h_hi, r_hi, preferred_element_type=F32) + jnp.dot(h_hi, r_lo, preferred_element_type=F32)
              + jnp.dot(h_lo, r_hi, preferred_element_type=F32))
    lane = lax.broadcasted_iota(jnp.int32, logits.shape, 1)
    neg = jnp.float32(-jnp.inf)
    lg = jnp.where(lane < n_experts, logits, neg)
    m1 = jnp.max(lg, axis=-1, keepdims=True)
    i1 = jnp.min(jnp.where(lg == m1, lane, LANES), axis=-1, keepdims=True)
    lg2 = jnp.where(lane == i1, neg, lg)
    m2 = jnp.max(lg2, axis=-1, keepdims=True)
    i2 = jnp.min(jnp.where(lg2 == m2, lane, LANES), axis=-1, keepdims=True)
    e2 = jnp.exp(m2 - m1)
    w1 = 1.0 / (1.0 + e2)
    w2 = e2 / (1.0 + e2)
    route_ref[...] = jnp.where(lane == 0, i1.astype(F32),
                               jnp.where(lane == 1, i2.astype(F32),
                                         jnp.where(lane == 2, w1, jnp.where(lane == 3, w2, 0.0))))


def _merge(od, og, gg, xall, w_out, onorm, g_post, g_pre_ffn, mod5, layer, router_pad, geom):
    t, d = xall.shape
    tm, tiles_per_b, ctx_row = geom
    n_experts = 0 if router_pad is None else router_pad[1]

    def mod_spec(j):
        return pl.BlockSpec((None, None, None, 1, d),
                            lambda i: (layer, j, _mod_row(i, tiles_per_b, ctx_row), 0, 0))

    row = lambda w: pl.BlockSpec((tm, w), lambda i: (i, 0))
    full = lambda a: pl.BlockSpec(a.shape, lambda i: (0,) * a.ndim)
    in_specs = [row(HEAD_W), row(HEAD_W), row(HEAD_W), row(d), full(w_out), full(onorm), full(g_post),
                mod_spec(2), full(g_pre_ffn), mod_spec(4), mod_spec(3)]
    args = [od, og, gg, xall, w_out, onorm, g_post, mod5, g_pre_ffn, mod5, mod5]
    out_shape = [jax.ShapeDtypeStruct((t, d), F32)]
    out_specs = [row(d)]
    if n_experts:
        in_specs.append(full(router_pad[0]))
        args.append(router_pad[0])
        out_shape += [jax.ShapeDtypeStruct((t * SUBLANES, LANES), F32), jax.ShapeDtypeStruct((t, LANES), F32)]
        out_specs += [pl.BlockSpec((tm * SUBLANES, LANES), lambda i: (i, 0)), row(LANES)]
    else:
        out_shape.append(jax.ShapeDtypeStruct((t, d), BF16))
        out_specs.append(row(d))
    return pl.pallas_call(
        functools.partial(_merge_kernel, n_experts=n_experts),
        out_shape=out_shape,
        grid=(t // tm,),
        in_specs=in_specs,
        out_specs=out_specs,
        compiler_params=_cparams(1, VMEM_LIMIT),
        name="merge_outproj",
    )(*args)


def _swiglu(xb, w1_ref, w3_ref, w2_ref):
    d_ff = w1_ref.shape[-1]
    cut = -(-(d_ff // 2) // MXU_DIM) * MXU_DIM
    y = None
    for c0, c1 in ((0, cut), (cut, d_ff)):
        a = jnp.dot(xb, w1_ref[:, c0:c1], preferred_element_type=F32)
        b = jnp.dot(xb, w3_ref[:, c0:c1], preferred_element_type=F32)
        m = (_silu(a) * b).astype(BF16)
        part = jnp.dot(m, w2_ref[c0:c1, :], preferred_element_type=F32)
        y = part if y is None else y + part
    return y


def _dense_ffn_kernel(h_ref, x_ref, w1_ref, w3_ref, w2_ref, gpost_ref, gt_ref, o_ref):
    y = _swiglu(h_ref[...], w1_ref, w3_ref, w2_ref)
    o_ref[...] = x_ref[...] + gt_ref[...] * _rms(y, gpost_ref[...])


def _dense_ffn(h, xall, w1, w3, w2, g_post, mod5, layer, geom):
    t, d = xall.shape
    tm, tiles_per_b, ctx_row = geom
    row = lambda w: pl.BlockSpec((tm, w), lambda i: (i, 0))
    resident = lambda a: pl.BlockSpec(a.shape, lambda i: (0,) * a.ndim, pipeline_mode=pl.Buffered(1))
    gt_spec = pl.BlockSpec((None, None, None, 1, d),
                           lambda i: (layer, 5, _mod_row(i, tiles_per_b, ctx_row), 0, 0))
    return pl.pallas_call(
        _dense_ffn_kernel,
        out_shape=jax.ShapeDtypeStruct((t, d), F32),
        grid=(t // tm,),
        in_specs=[row(d), row(d), resident(w1), resident(w3), resident(w2),
                  pl.BlockSpec(g_post.shape, lambda i: (0, 0)), gt_spec],
        out_specs=row(d),
        compiler_params=_cparams(1, VMEM_LIMIT),
        name="dense_ffn",
    )(h, xall, w1, w3, w2, g_post, mod5)


def _tile_copy(src, src_tok, dst, dst_tok, sem):
    s0 = pl.multiple_of(src_tok * SUBLANES, SUBLANES)
    d0 = pl.multiple_of(dst_tok * SUBLANES, SUBLANES)
    return pltpu.make_async_copy(src.at[pl.ds(s0, SUBLANES), :], dst.at[pl.ds(d0, SUBLANES), :], sem)


def _gather_step(src_hbm, idx_ref, idx_next_ref, buf, sem, n_copies, compute, valid=None, skipped=None):
    i = pl.program_id(0)
    slot = lax.rem(i, 2)

    def start(ref, s, r):
        _tile_copy(src_hbm, ref[0, 0, r], buf.at[s], r, sem.at[s]).start()

    def issue_loop(ref, s):
        lax.fori_loop(0, n_copies, lambda r, c: (start(ref, s, r), c)[1], 0, unroll=DMA_ISSUE_UNROLL)

    def wait_slot(s):
        def wait(r, carry):
            _tile_copy(src_hbm, 0, buf.at[s], r, sem.at[s]).wait()
            return carry
        lax.fori_loop(0, n_copies, wait, 0, unroll=DMA_ISSUE_UNROLL)

    def compute_and_issue():
        compute(buf.at[slot])
        for r in range(n_copies):
            start(idx_next_ref, 1 - slot, r)

    @pl.when(i == 0)
    def _():
        issue_loop(idx_ref, 0)

    wait_slot(slot)
    if valid is None:
        compute_and_issue()
    else:
        pl.when(valid)(compute_and_issue)

        @pl.when(jnp.logical_not(valid))
        def _():
            skipped()
            issue_loop(idx_next_ref, 1 - slot)

    @pl.when(i == pl.num_programs(0) - 1)
    def _():
        wait_slot(1 - slot)


def _moe_ffn_kernel(te_ref, tv_ref, rows_ref, rows_next_ref, h_hbm, w1_ref, w3_ref, w2_ref, y_ref,
                    xbuf, sem, *, tm):
    del te_ref

    def compute(x_tiles):
        _to_tiles(y_ref, _swiglu(_from_tiles(x_tiles, tm).astype(BF16), w1_ref, w3_ref, w2_ref))

    def skipped():
        y_ref[...] = jnp.zeros_like(y_ref)

    _gather_step(h_hbm, rows_ref, rows_next_ref, xbuf, sem, tm, compute,
                 valid=tv_ref[pl.program_id(0)] == 1, skipped=skipped)


def _moe_ffn(h_tiles, tile_expert, tile_valid, row_token, w1, w3, w2, tm):
    n_tiles = tile_expert.shape[0]
    _, d, d_ff = w1.shape
    rows = row_token.reshape(n_tiles, 1, tm)
    idx_spec = lambda f: pl.BlockSpec((1, 1, tm), lambda i, te, tv: (f(i), 0, 0), memory_space=pltpu.SMEM)
    grid_spec = pltpu.PrefetchScalarGridSpec(
        num_scalar_prefetch=2,
        grid=(n_tiles,),
        in_specs=[idx_spec(lambda i: i), idx_spec(lambda i: jnp.minimum(i + 1, n_tiles - 1)),
                  pl.BlockSpec(memory_space=pl.ANY),
                  pl.BlockSpec((None, d, d_ff), lambda i, te, tv: (te[i], 0, 0)),
                  pl.BlockSpec((None, d, d_ff), lambda i, te, tv: (te[i], 0, 0)),
                  pl.BlockSpec((None, d_ff, d), lambda i, te, tv: (te[i], 0, 0))],
        out_specs=pl.BlockSpec((tm * SUBLANES, LANES), lambda i, te, tv: (i, 0)),
        scratch_shapes=[pltpu.VMEM((2, tm * SUBLANES, LANES), F32), pltpu.SemaphoreType.DMA((2,))],
    )
    return pl.pallas_call(
        functools.partial(_moe_ffn_kernel, tm=tm),
        out_shape=jax.ShapeDtypeStruct((n_tiles * tm * SUBLANES, LANES), F32),
        grid_spec=grid_spec,
        compiler_params=_cparams(1, VMEM_LIMIT),
        name="moe_ffn",
    )(tile_expert, tile_valid, rows, rows, h_tiles, w1, w3, w2)


def _moe_combine_kernel(pos_ref, pos_next_ref, y_hbm, route_ref, x_ref, gpost_ref, gt_ref, o_ref, ybuf, sem):
    tm = x_ref.shape[0]

    def compute(y_tiles):
        route = route_ref[...]
        y = route[:, 2:3] * _from_tiles(y_tiles, tm) + route[:, 3:4] * _from_tiles(y_tiles, tm, tm)
        o_ref[...] = x_ref[...] + gt_ref[...] * _rms(y, gpost_ref[...])

    _gather_step(y_hbm, pos_ref, pos_next_ref, ybuf, sem, TOP_K * tm, compute)


def _moe_combine(y_tiles, pos_tiles, route, xall, g_post, mod5, layer, geom, latent_only):
    t, d = xall.shape
    tm, tiles_per_b, ctx_row = geom
    if latent_only:
        lat_tiles = tiles_per_b - 1
        n_steps = (t // tm // tiles_per_b) * lat_tiles
        tile = lambda i: (i // lat_tiles) * tiles_per_b + 1 + i % lat_tiles
    else:
        n_steps = t // tm
        tile = lambda i: i
    row = lambda w: pl.BlockSpec((tm, w), lambda i: (tile(i), 0))
    gt_spec = pl.BlockSpec((None, None, None, 1, d),
                           lambda i: (layer, 5, _mod_row(tile(i), tiles_per_b, ctx_row), 0, 0))
    pos_spec = lambda f: pl.BlockSpec((1, 1, TOP_K * tm), lambda i: (tile(f(i)), 0, 0), memory_space=pltpu.SMEM)
    return pl.pallas_call(
        _moe_combine_kernel,
        out_shape=jax.ShapeDtypeStruct((n_steps * tm, d), F32),
        grid=(n_steps,),
        in_specs=[pos_spec(lambda i: i), pos_spec(lambda i: jnp.minimum(i + 1, n_steps - 1)),
                  pl.BlockSpec(memory_space=pl.ANY),
                  row(LANES), row(d), pl.BlockSpec(g_post.shape, lambda i: (0, 0)), gt_spec],
        out_specs=pl.BlockSpec((tm, d), lambda i: (i, 0)),
        scratch_shapes=[pltpu.VMEM((2, TOP_K * tm * SUBLANES, LANES), F32), pltpu.SemaphoreType.DMA((2,))],
        compiler_params=_cparams(1, VMEM_LIMIT),
        name="moe_combine",
    )(pos_tiles, pos_tiles, y_tiles, route, xall, g_post, mod5)


def _route_plan(route, tm, n_experts, tiles_per_b, latent_only):
    t = route.shape[0]
    e_tk = route[:, :TOP_K].astype(jnp.int32)
    n_routed = t
    if latent_only:
        is_ctx = (jnp.arange(t, dtype=jnp.int32) // tm) % tiles_per_b == 0
        e_tk = jnp.where(is_ctx[:, None], n_experts, e_tk)
        n_routed = t // tiles_per_b * (tiles_per_b - 1)
    n_tiles = (TOP_K * n_routed) // tm + n_experts
    e_flat = e_tk.T.reshape(-1)
    onehot = (e_flat[:, None] == jnp.arange(n_experts, dtype=jnp.int32)[None, :]).astype(jnp.int32)
    csum = jnp.cumsum(onehot, axis=0)
    counts = csum[-1]
    rank = jnp.sum(csum * onehot, axis=1) - 1
    group_tiles = (counts + tm - 1) // tm
    group_end = jnp.cumsum(group_tiles)
    group_start = (group_end - group_tiles) * tm
    pos = jnp.sum(onehot * group_start[None, :], axis=1) + rank
    tile_group = jnp.sum((jnp.arange(n_tiles, dtype=jnp.int32)[:, None] >= group_end[None, :]).astype(jnp.int32), axis=1)
    tile_valid = (tile_group < n_experts).astype(jnp.int32)
    tile_expert = jnp.minimum(tile_group, n_experts - 1)
    token = jnp.tile(jnp.arange(t, dtype=jnp.int32), TOP_K)
    routed = e_flat < n_experts
    scatter_row = jnp.where(routed, pos, n_tiles * tm + jnp.arange(TOP_K * t, dtype=jnp.int32))
    row_token = jnp.zeros((n_tiles * tm,), jnp.int32).at[scatter_row].set(token, unique_indices=True, mode="drop")
    pos = jnp.maximum(pos, 0)
    pos_tiles = pos.reshape(TOP_K, t // tm, tm).transpose(1, 0, 2).reshape(t // tm, 1, TOP_K * tm)
    return tile_expert, tile_valid, row_token, pos_tiles


def _rope_tables(seq, ctx_len):
    rows_n = seq // GRID_W
    row = jnp.repeat(jnp.arange(rows_n), GRID_W).astype(F32)
    col = jnp.tile(jnp.arange(GRID_W), rows_n).astype(F32)
    n_freq = QK_DIM // 4
    inv = ROPE_THETA ** (-jnp.arange(n_freq, dtype=F32) / n_freq)
    ang = jnp.concatenate([row[:, None] * inv, row[:, None] * inv, col[:, None] * inv, col[:, None] * inv], axis=1)
    cos, sin = jnp.cos(ang), jnp.sin(ang)
    first = (jnp.arange(QK_DIM) % (2 * n_freq)) < n_freq
    s1 = jnp.where(first[None, :], -sin, 0.0)
    s2 = jnp.where(first[None, :], 0.0, sin)
    pad = lambda a, v: jnp.concatenate([jnp.full((ctx_len, LANES), v, F32), jnp.tile(a, (1, 2))], axis=0)
    return pad(cos, 1.0), pad(s1, 0.0), pad(s2, 0.0)


def kernel(x, c, ctx, c_ctx, w_mod, b_mod, g_pre_mix, g_post_mix, g_pre_ffn, g_post_ffn, w_in, gla_gate_w2, gla_gate_b, diff_lambda, diff_subln, gla_onorm, w_out, ffn_w1, ffn_w3, ffn_w2, router, moe_w1, moe_w3, moe_w2):
    batch, seq, d = x.shape
    ctx_len = ctx.shape[1]
    depth = w_in.shape[0]
    n_experts = router.shape[-1]
    assert seq % ctx_len == 0 and ctx_len % GLA_CHUNK == 0 and d == 2 * HEAD_W == SUBLANES * LANES
    assert w_in.shape[-1] == _OFF_GLR + 2 * GATE_RANK
    tm = ctx_len
    tiles_per_b = (ctx_len + seq) // tm
    mod_rows = -(-(batch + 1) // SUBLANES) * SUBLANES
    geom = (tm, tiles_per_b, batch)

    cc = jnp.concatenate([c, c_ctx[None, :], jnp.zeros((mod_rows - batch - 1, d), F32)], axis=0)
    mod = _modulation(cc, w_mod, b_mod)
    mod5 = mod.reshape(depth, N_MOD, mod_rows, 1, d)
    rope_tabs = _rope_tables(seq, ctx_len)
    xall = jnp.concatenate([ctx, x], axis=1).reshape(batch * (ctx_len + seq), d)
    row2 = lambda a: a.reshape(1, -1)

    for l in range(depth):
        last = l == depth - 1
        lam_init = 0.8 - 0.6 * math.exp(-0.3 * l)
        w_main = w_in[l, :, :_OFF_GLR].astype(BF16)
        w_glr = jnp.pad(w_in[l, :, _OFF_GLR:], ((0, 0), (0, LANES - 2 * GATE_RANK))).astype(BF16)
        w2bd = jnp.zeros((LANES, 2 * GLA_KW), F32)
        w2bd = w2bd.at[:GATE_RANK, :GLA_KW].set(gla_gate_w2[l, 0])
        w2bd = w2bd.at[GATE_RANK:2 * GATE_RANK, GLA_KW:].set(gla_gate_w2[l, 1])
        w2bd = jnp.stack(_split2(w2bd))
        gate_b = gla_gate_b[l].reshape(1, 2 * GLA_KW)

        dq, dk, dv, gq, gk, gv, gg, gf, gb = _inproj(
            xall, row2(g_pre_mix[l]), mod5, l, w_main, w_glr, w2bd, gate_b, rope_tabs, geom)
        od = _diff_attn(dq, dk, dv, diff_lambda[l], row2(diff_subln[l]), lam_init, not last, geom, batch)
        og = _gla(gq, gk, gv, gf, gb, geom, batch)

        moe = l % 2 == 1
        idx = l // 2
        router_pad = None
        if moe:
            router_pad = (jnp.pad(router[idx], ((0, 0), (0, LANES - n_experts))), n_experts)
        outs = _merge(od, og, gg, xall, w_out[l].astype(BF16), row2(gla_onorm[l]), row2(g_post_mix[l]),
                      row2(g_pre_ffn[l]), mod5, l, router_pad, geom)
        if moe:
            xall, h_tiles, route = outs
            tile_expert, tile_valid, row_token, pos_tiles = _route_plan(route, tm, n_experts, tiles_per_b, last)
            y_tiles = _moe_ffn(h_tiles, tile_expert, tile_valid, row_token, moe_w1[idx].astype(BF16),
                               moe_w3[idx].astype(BF16), moe_w2[idx].astype(BF16), tm)
            xall = _moe_combine(y_tiles, pos_tiles, route, xall, row2(g_post_ffn[l]), mod5, l, geom, last)
        else:
            xall, h = outs
            xall = _dense_ffn(h, xall, ffn_w1[idx].astype(BF16), ffn_w3[idx].astype(BF16),
                              ffn_w2[idx].astype(BF16), row2(g_post_ffn[l]), mod5, l, geom)

    if xall.shape[0] == batch * seq:
        return xall.reshape(batch, seq, d)
    return xall.reshape(batch, ctx_len + seq, d)[:, ctx_len:, :]
```

```python
import functools
import math

import jax
import jax.numpy as jnp
from jax import lax
from jax.experimental import pallas as pl
from jax.experimental.pallas import tpu as pltpu

F32 = jnp.float32
BF16 = jnp.bfloat16
HIGHEST = lax.Precision.HIGHEST

N_HEADS = 4
QK_DIM = 64
V_DIM = 128
HEAD_W = N_HEADS * V_DIM
GLA_KW = N_HEADS * QK_DIM
GATE_RANK = 16
GATE_NORMALIZER = 16.0
GLA_CHUNK = 64
GRID_W = 64
ROPE_THETA = 10000.0
N_MOD = 6
TOP_K = 2
EPS = 1e-6
LOG2_E = 1.4426950408889634
LANES = 128
SUBLANES = 8
MXU_DIM = 256
VMEM_LIMIT = 56 * 1024 * 1024
DMA_ISSUE_UNROLL = 8

_OFF_DQ, _OFF_DK, _OFF_DV = 0, 512, 1024
_OFF_GQ, _OFF_GK, _OFF_GV, _OFF_GG, _OFF_GLR = 1536, 1792, 2048, 2560, 3072


def _rms(x, g):
    ms = jnp.mean(x * x, axis=-1, keepdims=True)
    return x * lax.rsqrt(ms + EPS) * g


def _silu(x):
    return x * (1.0 / (1.0 + jnp.exp(-x)))


def _split2(a):
    hi = a.astype(BF16)
    return hi, (a - hi.astype(F32)).astype(BF16)


def _cparams(n_axes, vmem=None):
    return pltpu.CompilerParams(dimension_semantics=("arbitrary",) * n_axes,
                                vmem_limit_bytes=vmem)


def _mod_row(i, tiles_per_b, ctx_row):
    return jnp.where(i % tiles_per_b == 0, ctx_row, i // tiles_per_b)


def _to_tiles(ref, val, row0=0):
    rows = val.shape[0]
    for c in range(SUBLANES):
        ref[pl.ds(row0 * SUBLANES + c, rows, stride=SUBLANES), :] = val[:, LANES * c:LANES * (c + 1)]


def _from_tiles(ref, rows, row0=0):
    return jnp.concatenate(
        [ref[pl.ds(row0 * SUBLANES + c, rows, stride=SUBLANES), :] for c in range(SUBLANES)], axis=1)


def _mod_kernel(c_ref, w_ref, b_ref, o_ref):
    o_ref[...] = jnp.dot(_silu(c_ref[...]), w_ref[...], precision=HIGHEST,
                         preferred_element_type=F32) + b_ref[...]


def _modulation(cc, w_mod, b_mod):
    n_layers, d, _ = w_mod.shape
    r = cc.shape[0]
    b4 = b_mod.reshape(n_layers, N_MOD, 1, d)
    return pl.pallas_call(
        _mod_kernel,
        out_shape=jax.ShapeDtypeStruct((n_layers, N_MOD, r, d), F32),
        grid=(n_layers, N_MOD),
        in_specs=[pl.BlockSpec((r, d), lambda l, j: (0, 0)),
                  pl.BlockSpec((None, d, d), lambda l, j: (l, 0, j)),
                  pl.BlockSpec((None, None, 1, d), lambda l, j: (l, j, 0, 0))],
        out_specs=pl.BlockSpec((None, None, r, d), lambda l, j: (l, j, 0, 0)),
        compiler_params=_cparams(2),
        name="modulation",
    )(cc, w_mod, b4)


def _inproj_kernel(x_ref, g_ref, sc_ref, sh_ref, w_ref, wglr_ref, w2_ref, gb_ref, tri_ref,
                   cos_ref, s1_ref, s2_ref,
                   dq_ref, dk_ref, dv_ref, gq_ref, gk_ref, gv_ref, gg_ref, gf_ref, gb_out_ref):
    h = _rms(x_ref[...], g_ref[...]) * (1.0 + sc_ref[...]) + sh_ref[...]
    hb = h.astype(BF16)
    cos, s1, s2 = cos_ref[...], s1_ref[...], s2_ref[...]

    def proj(off, n):
        return jnp.dot(hb, w_ref[:, off:off + n], preferred_element_type=F32)

    def rope(p):
        outs = []
        for j in range(N_HEADS):
            xg = p[:, LANES * j:LANES * (j + 1)]
            outs.append(xg * cos + pltpu.roll(xg, LANES - 16, 1) * s1 + pltpu.roll(xg, 16, 1) * s2)
        return jnp.concatenate(outs, axis=1)

    scale = QK_DIM ** -0.5
    dq_ref[...] = (rope(proj(_OFF_DQ, HEAD_W)) * (scale * LOG2_E)).astype(BF16)
    dk_ref[...] = rope(proj(_OFF_DK, HEAD_W)).astype(BF16)
    dv_ref[...] = proj(_OFF_DV, HEAD_W).astype(BF16)
    gq_ref[...] = proj(_OFF_GQ, GLA_KW) * scale
    gk_ref[...] = proj(_OFF_GK, GLA_KW)
    gv_ref[...] = proj(_OFF_GV, HEAD_W)
    gg_ref[...] = proj(_OFF_GG, HEAD_W)
    glr = jnp.dot(hb, wglr_ref[...], preferred_element_type=F32)
    glr_hi, glr_lo = _split2(glr)
    w2_hi, w2_lo = w2_ref[0], w2_ref[1]
    z = (jnp.dot(glr_hi, w2_hi, preferred_element_type=F32) + jnp.dot(glr_hi, w2_lo, preferred_element_type=F32)
         + jnp.dot(glr_lo, w2_hi, preferred_element_type=F32)) + gb_ref[...]
    gl = (jnp.minimum(z, 0.0) - jnp.log1p(jnp.exp(-jnp.abs(z)))) * (1.0 / GATE_NORMALIZER)

    def tri_sum(tri, a):
        hi, lo = _split2(a)
        return jnp.dot(tri, hi, preferred_element_type=F32) + jnp.dot(tri, lo, preferred_element_type=F32)

    gf_ref[...] = tri_sum(tri_ref[0], gl[:, :GLA_KW])
    gb_out_ref[...] = tri_sum(tri_ref[1], gl[:, GLA_KW:])


def _chunk_tri(tm):
    r = jnp.arange(tm)[:, None]
    c = jnp.arange(tm)[None, :]
    same = (r // GLA_CHUNK) == (c // GLA_CHUNK)
    return jnp.stack([same & (r >= c), same & (c >= r)]).astype(BF16)


def _inproj(xall, g_pre, mod5, layer, w_main, w_glr, w2bd, gate_b, rope_tabs, geom):
    t, d = xall.shape
    tm, tiles_per_b, ctx_row = geom
    n_tiles = t // tm
    tri = _chunk_tri(tm)

    def mod_spec(j):
        return pl.BlockSpec((None, None, None, 1, d),
                            lambda i: (layer, j, _mod_row(i, tiles_per_b, ctx_row), 0, 0))

    row = lambda w: pl.BlockSpec((tm, w), lambda i: (i, 0))
    full = lambda a: pl.BlockSpec(a.shape, lambda i: (0,) * a.ndim)
    tab = pl.BlockSpec((tm, LANES), lambda i: (i % tiles_per_b, 0))
    widths = (HEAD_W, HEAD_W, HEAD_W, GLA_KW, GLA_KW, HEAD_W, HEAD_W, GLA_KW, GLA_KW)
    dtypes = (BF16, BF16, BF16, F32, F32, F32, F32, F32, F32)
    return pl.pallas_call(
        _inproj_kernel,
        out_shape=[jax.ShapeDtypeStruct((t, w), dt) for w, dt in zip(widths, dtypes)],
        grid=(n_tiles,),
        in_specs=[row(d), full(g_pre), mod_spec(1), mod_spec(0), full(w_main), full(w_glr),
                  full(w2bd), full(gate_b), full(tri), tab, tab, tab],
        out_specs=[row(w) for w in widths],
        compiler_params=_cparams(1, VMEM_LIMIT),
        name="inproj",
    )(xall, g_pre, mod5, mod5, w_main, w_glr, w2bd, gate_b, tri, *rope_tabs)


ATTN_HEADS_PER_STEP = 4


def _diff_attn_kernel(q_ref, k_ref, v_ref, lam_ref, g_ref, o_ref, *, lam_init, ctx_len, with_ctx):
    lv = lam_ref[...]
    lam = (jnp.exp(jnp.sum(lv[0:1] * lv[1:2], axis=-1, keepdims=True))
           - jnp.exp(jnp.sum(lv[2:3] * lv[3:4], axis=-1, keepdims=True)) + lam_init)
    lane = lax.broadcasted_iota(jnp.int32, (1, LANES), 1)

    def attend(n_keys):
        ones = jnp.ones((n_keys, V_DIM), BF16)
        for hh in range(ATTN_HEADS_PER_STEP):
            cols = slice(V_DIM * hh, V_DIM * (hh + 1))
            q = q_ref[:, cols]
            k = k_ref[0:n_keys, cols]
            v_ext = jnp.concatenate([v_ref[0:n_keys, cols], ones], axis=1)

            def one(qm):
                s = lax.dot_general(qm, k, (((1,), (1,)), ((), ())), preferred_element_type=F32)
                p = jnp.exp2(s - jnp.max(s, axis=-1, keepdims=True))
                r = jnp.dot(p.astype(BF16), v_ext, preferred_element_type=F32)
                return r[:, :V_DIM] / r[:, V_DIM:V_DIM + 1]

            o = (one(jnp.where(lane < QK_DIM, q, jnp.zeros_like(q)))
                 - lam * one(jnp.where(lane >= QK_DIM, q, jnp.zeros_like(q))))
            o_ref[:, cols] = (_rms(o, g_ref[...]) * (1.0 - lam_init)).astype(o_ref.dtype)

    is_ctx = pl.program_id(2) == 0

    @pl.when(jnp.logical_not(is_ctx))
    def _():
        attend(k_ref.shape[0])

    @pl.when(is_ctx)
    def _():
        if with_ctx:
            attend(ctx_len)
        else:
            o_ref[...] = jnp.zeros_like(o_ref)


def _diff_attn(dq, dk, dv, lam_vec, subln_g, lam_init, with_ctx, geom, batch):
    t = dq.shape[0]
    tm, tiles_per_b, _ = geom
    seg = tm * tiles_per_b
    w = ATTN_HEADS_PER_STEP * V_DIM
    dk3 = dk.reshape(batch, seg, HEAD_W)
    dv3 = dv.reshape(batch, seg, HEAD_W)
    kern = functools.partial(_diff_attn_kernel, lam_init=lam_init, ctx_len=tm, with_ctx=with_ctx)
    return pl.pallas_call(
        kern,
        out_shape=jax.ShapeDtypeStruct((t, HEAD_W), BF16),
        grid=(batch, N_HEADS // ATTN_HEADS_PER_STEP, tiles_per_b),
        in_specs=[pl.BlockSpec((tm, w), lambda b, h, i: (b * tiles_per_b + i, h)),
                  pl.BlockSpec((None, seg, w), lambda b, h, i: (b, 0, h)),
                  pl.BlockSpec((None, seg, w), lambda b, h, i: (b, 0, h)),
                  pl.BlockSpec(lam_vec.shape, lambda b, h, i: (0, 0)),
                  pl.BlockSpec(subln_g.shape, lambda b, h, i: (0, 0))],
        out_specs=pl.BlockSpec((tm, w), lambda b, h, i: (b * tiles_per_b + i, h)),
        compiler_params=_cparams(3, VMEM_LIMIT),
        name="diff_attn",
    )(dq, dk3, dv3, lam_vec, subln_g)


GLA_UNROLL = 4


def _gla_kernel(q_ref, k_ref, v_ref, gf_ref, gb_ref, o_ref, *, n_chunks, n_ctx_chunks):
    c = GLA_CHUNK
    r_i = lax.broadcasted_iota(jnp.int32, (N_HEADS * c, c), 0) % c
    c_i = lax.broadcasted_iota(jnp.int32, (N_HEADS * c, c), 1)
    lane = lax.broadcasted_iota(jnp.int32, (1, GLA_KW), 1)
    head_masks = [((lane // QK_DIM) == h).astype(F32) for h in range(N_HEADS)]
    causal = (r_i >= c_i, c_i >= r_i)
    last_row = (c - 1, 0)
    mid_row = (c // 2 - 1, c - c // 2)
    g_refs = (gf_ref, gb_ref)
    nt = (((1,), (1,)), ((), ()))
    tn = (((0,), (0,)), ((), ()))

    def head_stack(a):
        return jnp.concatenate([(a * head_masks[h]).astype(BF16) for h in range(N_HEADS)], axis=0)

    def chunk(direction, chunk_idx, st):
        rows = pl.ds(pl.multiple_of(chunk_idx * c, c), c)
        q, k, v, g = q_ref[rows, :], k_ref[rows, :], v_ref[rows, :], g_refs[direction][rows, :]
        last, mid = last_row[direction], mid_row[direction]
        g_last = g[last:last + 1, :]
        g_mid = g[mid:mid + 1, :]
        ke = (k * jnp.exp(g_mid - g)).astype(BF16)
        vb = v.astype(BF16)
        att = lax.dot_general(head_stack(q * jnp.exp(g - g_mid)), ke, nt, preferred_element_type=F32)
        att = jnp.where(causal[direction], att, 0.0).astype(BF16)
        inter = lax.dot_general(head_stack(q * jnp.exp(g)), st.astype(BF16), nt,
                                preferred_element_type=F32)
        o = jnp.concatenate(
            [jnp.dot(att[c * h:c * (h + 1), :], vb[:, V_DIM * h:V_DIM * (h + 1)], preferred_element_type=F32)
             + inter[c * h:c * (h + 1), :] for h in range(N_HEADS)], axis=1)
        o_ref[rows, :] = o_ref[rows, :] + o
        v_stack = jnp.concatenate([vb[:, V_DIM * h:V_DIM * (h + 1)] for h in range(N_HEADS)], axis=0)
        ds_t = lax.dot_general(v_stack, head_stack(k * jnp.exp(g_last - g)), tn,
                               preferred_element_type=F32)
        return st * jnp.exp(g_last) + ds_t

    o_ref[...] = jnp.zeros_like(o_ref)

    def body(i, carry):
        st_f, st_b = carry
        i_b = jnp.where(i < n_ctx_chunks, n_ctx_chunks - 1 - i, n_chunks - 1 - (i - n_ctx_chunks))
        return chunk(0, i, st_f), chunk(1, i_b, st_b)

    zero = jnp.zeros((V_DIM, GLA_KW), F32)
    lax.fori_loop(0, n_chunks, body, (zero, zero), unroll=GLA_UNROLL)


def _gla(gq, gk, gv, gf, gb, geom, batch):
    tm, tiles_per_b, _ = geom
    seg = tm * tiles_per_b
    r3 = lambda a: a.reshape(batch, seg, a.shape[-1])
    spec = lambda w: pl.BlockSpec((None, seg, w), lambda b: (b, 0, 0))
    kern = functools.partial(_gla_kernel, n_chunks=seg // GLA_CHUNK, n_ctx_chunks=tm // GLA_CHUNK)
    out = pl.pallas_call(
        kern,
        out_shape=jax.ShapeDtypeStruct((batch, seg, HEAD_W), F32),
        grid=(batch,),
        in_specs=[spec(GLA_KW), spec(GLA_KW), spec(HEAD_W), spec(GLA_KW), spec(GLA_KW)],
        out_specs=spec(HEAD_W),
        compiler_params=_cparams(1, VMEM_LIMIT),
        name="gla",
    )(r3(gq), r3(gk), r3(gv), r3(gf), r3(gb))
    return out.reshape(batch * seg, HEAD_W)


MERGE_ROW_SPLITS = 1
def _merge_kernel(od_ref, og_ref, gg_ref, x_ref, wout_ref, onorm_ref, gpost_ref, gt_ref,
                  gpre_ref, sc_ref, sh_ref, *rest, n_experts):
    if n_experts:
        router_ref, xo_ref, h_ref, route_ref = rest
    else:
        xo_ref, h_ref = rest
    onorm = onorm_ref[...]
    n_rows = x_ref.shape[0] // MERGE_ROW_SPLITS
    for r0 in range(0, x_ref.shape[0], n_rows):
        rs = pl.ds(r0, n_rows)
        og = og_ref[rs, :]
        ogn = jnp.concatenate([_rms(og[:, V_DIM * h:V_DIM * (h + 1)], onorm) for h in range(N_HEADS)], axis=1)
        ogn = (ogn * _silu(gg_ref[rs, :])).astype(BF16)
        y = (jnp.dot(od_ref[rs, :], wout_ref[0:HEAD_W, :], preferred_element_type=F32)
             + jnp.dot(ogn, wout_ref[HEAD_W:2 * HEAD_W, :], preferred_element_type=F32))
        x_new = x_ref[rs, :] + gt_ref[...] * _rms(y, gpost_ref[...])
        xo_ref[rs, :] = x_new
        h = _rms(x_new, gpre_ref[...]) * (1.0 + sc_ref[...]) + sh_ref[...]
        if not n_experts:
            h_ref[rs, :] = h.astype(h_ref.dtype)
            continue
        _to_tiles(h_ref, h, r0)
        h_hi, h_lo = _split2(h)
        r_hi, r_lo = _split2(router_ref[...])
        logits = (jnp.dot(h_hi, r_hi, preferred_element_type=F32) + jnp.dot(h_hi, r_lo, preferred_element_type=F32)
                  + jnp.dot(h_lo, r_hi, preferred_element_type=F32))
        lane = lax.broadcasted_iota(jnp.int32, logits.shape, 1)
        neg = jnp.float32(-jnp.inf)
        lg = jnp.where(lane < n_experts, logits, neg)
        m1 = jnp.max(lg, axis=-1, keepdims=True)
        i1 = jnp.min(jnp.where(lg == m1, lane, LANES), axis=-1, keepdims=True)
        lg2 = jnp.where(lane == i1, neg, lg)
        m2 = jnp.max(lg2, axis=-1, keepdims=True)
        i2 = jnp.min(jnp.where(lg2 == m2, lane, LANES), axis=-1, keepdims=True)
        e2 = jnp.exp(m2 - m1)
        w1 = 1.0 / (1.0 + e2)
        w2 = e2 / (1.0 + e2)
        route_ref[rs, :] = jnp.where(lane == 0, i1.astype(F32),
                                     jnp.where(lane == 1, i2.astype(F32),
                                               jnp.where(lane == 2, w1, jnp.where(lane == 3, w2, 0.0))))


def _merge(od, og, gg, xall, w_out, onorm, g_post, g_pre_ffn, mod5, layer, router_pad, geom):
    t, d = xall.shape
    tm, tiles_per_b, ctx_row = geom
    n_experts = 0 if router_pad is None else router_pad[1]

    def mod_spec(j):
        return pl.BlockSpec((None, None, None, 1, d),
                            lambda i: (layer, j, _mod_row(i, tiles_per_b, ctx_row), 0, 0))

    row = lambda w: pl.BlockSpec((tm, w), lambda i: (i, 0))
    full = lambda a: pl.BlockSpec(a.shape, lambda i: (0,) * a.ndim)
    in_specs = [row(HEAD_W), row(HEAD_W), row(HEAD_W), row(d), full(w_out), full(onorm), full(g_post),
                mod_spec(2), full(g_pre_ffn), mod_spec(4), mod_spec(3)]
    args = [od, og, gg, xall, w_out, onorm, g_post, mod5, g_pre_ffn, mod5, mod5]
    out_shape = [jax.ShapeDtypeStruct((t, d), F32)]
    out_specs = [row(d)]
    if n_experts:
        in_specs.append(full(router_pad[0]))
        args.append(router_pad[0])
        out_shape += [jax.ShapeDtypeStruct((t * SUBLANES, LANES), F32), jax.ShapeDtypeStruct((t, LANES), F32)]
        out_specs += [pl.BlockSpec((tm * SUBLANES, LANES), lambda i: (i, 0)), row(LANES)]
    else:
        out_shape.append(jax.ShapeDtypeStruct((t, d), BF16))
        out_specs.append(row(d))
    return pl.pallas_call(
        functools.partial(_merge_kernel, n_experts=n_experts),
        out_shape=out_shape,
        grid=(t // tm,),
        in_specs=in_specs,
        out_specs=out_specs,
        compiler_params=_cparams(1, VMEM_LIMIT),
        name="merge_outproj",
    )(*args)


def _swiglu(xb, w1_ref, w3_ref, w2_ref):
    d_ff = w1_ref.shape[-1]
    cut = -(-(d_ff // 2) // MXU_DIM) * MXU_DIM
    y = None
    for c0, c1 in ((0, cut), (cut, d_ff)):
        a = jnp.dot(xb, w1_ref[:, c0:c1], preferred_element_type=F32)
        b = jnp.dot(xb, w3_ref[:, c0:c1], preferred_element_type=F32)
        m = (_silu(a) * b).astype(BF16)
        part = jnp.dot(m, w2_ref[c0:c1, :], preferred_element_type=F32)
        y = part if y is None else y + part
    return y


def _dense_ffn_kernel(h_ref, x_ref, w1_ref, w3_ref, w2_ref, gpost_ref, gt_ref, o_ref):
    y = _swiglu(h_ref[...], w1_ref, w3_ref, w2_ref)
    o_ref[...] = x_ref[...] + gt_ref[...] * _rms(y, gpost_ref[...])


def _dense_ffn(h, xall, w1, w3, w2, g_post, mod5, layer, geom):
    t, d = xall.shape
    tm, tiles_per_b, ctx_row = geom
    row = lambda w: pl.BlockSpec((tm, w), lambda i: (i, 0))
    resident = lambda a: pl.BlockSpec(a.shape, lambda i: (0,) * a.ndim, pipeline_mode=pl.Buffered(1))
    gt_spec = pl.BlockSpec((None, None, None, 1, d),
                           lambda i: (layer, 5, _mod_row(i, tiles_per_b, ctx_row), 0, 0))
    return pl.pallas_call(
        _dense_ffn_kernel,
        out_shape=jax.ShapeDtypeStruct((t, d), F32),
        grid=(t // tm,),
        in_specs=[row(d), row(d), resident(w1), resident(w3), resident(w2),
                  pl.BlockSpec(g_post.shape, lambda i: (0, 0)), gt_spec],
        out_specs=row(d),
        compiler_params=_cparams(1, VMEM_LIMIT),
        name="dense_ffn",
    )(h, xall, w1, w3, w2, g_post, mod5)


def _tile_copy(src, src_tok, dst, dst_tok, sem):
    s0 = pl.multiple_of(src_tok * SUBLANES, SUBLANES)
    d0 = pl.multiple_of(dst_tok * SUBLANES, SUBLANES)
    return pltpu.make_async_copy(src.at[pl.ds(s0, SUBLANES), :], dst.at[pl.ds(d0, SUBLANES), :], sem)


GATHER_SLOTS = 3


def _gather_step(src_hbm, idx_refs, buf, sem, n_copies, compute, valid=None, skipped=None):
    idx_ref, idx_next_ref, idx_next2_ref = idx_refs
    i = pl.program_id(0)
    slot = lax.rem(i, GATHER_SLOTS)
    slot1 = lax.rem(i + 1, GATHER_SLOTS)
    slot2 = lax.rem(i + 2, GATHER_SLOTS)

    def start(ref, s, r):
        _tile_copy(src_hbm, ref[0, 0, r], buf.at[s], r, sem.at[s]).start()

    def issue_loop(ref, s):
        lax.fori_loop(0, n_copies, lambda r, c: (start(ref, s, r), c)[1], 0, unroll=DMA_ISSUE_UNROLL)

    def wait_slot(s):
        def wait(r, carry):
            _tile_copy(src_hbm, 0, buf.at[s], r, sem.at[s]).wait()
            return carry
        lax.fori_loop(0, n_copies, wait, 0, unroll=DMA_ISSUE_UNROLL)

    def compute_and_issue():
        compute(buf.at[slot])
        for r in range(n_copies):
            start(idx_next2_ref, slot2, r)

    @pl.when(i == 0)
    def _():
        issue_loop(idx_ref, 0)
        issue_loop(idx_next_ref, 1)

    wait_slot(slot)
    if valid is None:
        compute_and_issue()
    else:
        pl.when(valid)(compute_and_issue)

        @pl.when(jnp.logical_not(valid))
        def _():
            skipped()
            issue_loop(idx_next2_ref, slot2)

    @pl.when(i == pl.num_programs(0) - 1)
    def _():
        wait_slot(slot1)
        wait_slot(slot2)


def _moe_ffn_kernel(te_ref, tv_ref, rows_ref, rows1_ref, rows2_ref, h_hbm, w1_ref, w3_ref, w2_ref, y_ref,
                    xbuf, sem, *, tm):
    del te_ref

    def compute(x_tiles):
        _to_tiles(y_ref, _swiglu(_from_tiles(x_tiles, tm).astype(BF16), w1_ref, w3_ref, w2_ref))

    def skipped():
        y_ref[...] = jnp.zeros_like(y_ref)

    _gather_step(h_hbm, (rows_ref, rows1_ref, rows2_ref), xbuf, sem, tm, compute,
                 valid=tv_ref[pl.program_id(0)] == 1, skipped=skipped)


def _ahead_specs(block, n_steps, tile=lambda i: i):
    def spec(k):
        return pl.BlockSpec((1, 1, block), lambda i, *_: (tile(jnp.minimum(i + k, n_steps - 1)), 0, 0),
                            memory_space=pltpu.SMEM)
    return [spec(k) for k in range(GATHER_SLOTS)]


def _moe_ffn(h_tiles, tile_expert, tile_valid, row_token, w1, w3, w2, tm):
    n_tiles = tile_expert.shape[0]
    _, d, d_ff = w1.shape
    rows = row_token.reshape(n_tiles, 1, tm)
    grid_spec = pltpu.PrefetchScalarGridSpec(
        num_scalar_prefetch=2,
        grid=(n_tiles,),
        in_specs=_ahead_specs(tm, n_tiles) + [
            pl.BlockSpec(memory_space=pl.ANY),
            pl.BlockSpec((None, d, d_ff), lambda i, te, tv: (te[i], 0, 0)),
            pl.BlockSpec((None, d, d_ff), lambda i, te, tv: (te[i], 0, 0)),
            pl.BlockSpec((None, d_ff, d), lambda i, te, tv: (te[i], 0, 0))],
        out_specs=pl.BlockSpec((tm * SUBLANES, LANES), lambda i, te, tv: (i, 0)),
        scratch_shapes=[pltpu.VMEM((GATHER_SLOTS, tm * SUBLANES, LANES), F32),
                        pltpu.SemaphoreType.DMA((GATHER_SLOTS,))],
    )
    return pl.pallas_call(
        functools.partial(_moe_ffn_kernel, tm=tm),
        out_shape=jax.ShapeDtypeStruct((n_tiles * tm * SUBLANES, LANES), F32),
        grid_spec=grid_spec,
        compiler_params=_cparams(1, VMEM_LIMIT),
        name="moe_ffn",
    )(tile_expert, tile_valid, rows, rows, rows, h_tiles, w1, w3, w2)


def _moe_combine_kernel(pos_ref, pos1_ref, pos2_ref, y_hbm, route_ref, x_ref, gpost_ref, gt_ref, o_ref,
                        ybuf, sem):
    tm = x_ref.shape[0]

    def compute(y_tiles):
        route = route_ref[...]
        y = route[:, 2:3] * _from_tiles(y_tiles, tm) + route[:, 3:4] * _from_tiles(y_tiles, tm, tm)
        o_ref[...] = x_ref[...] + gt_ref[...] * _rms(y, gpost_ref[...])

    _gather_step(y_hbm, (pos_ref, pos1_ref, pos2_ref), ybuf, sem, TOP_K * tm, compute)


def _moe_combine(y_tiles, pos_tiles, route, xall, g_post, mod5, layer, geom, latent_only):
    t, d = xall.shape
    tm, tiles_per_b, ctx_row = geom
    if latent_only:
        lat_tiles = tiles_per_b - 1
        n_steps = (t // tm // tiles_per_b) * lat_tiles
        tile = lambda i: (i // lat_tiles) * tiles_per_b + 1 + i % lat_tiles
    else:
        n_steps = t // tm
        tile = lambda i: i
    row = lambda w: pl.BlockSpec((tm, w), lambda i: (tile(i), 0))
    gt_spec = pl.BlockSpec((None, None, None, 1, d),
                           lambda i: (layer, 5, _mod_row(tile(i), tiles_per_b, ctx_row), 0, 0))
    return pl.pallas_call(
        _moe_combine_kernel,
        out_shape=jax.ShapeDtypeStruct((n_steps * tm, d), F32),
        grid=(n_steps,),
        in_specs=_ahead_specs(TOP_K * tm, n_steps, tile) + [
            pl.BlockSpec(memory_space=pl.ANY),
            row(LANES), row(d), pl.BlockSpec(g_post.shape, lambda i: (0, 0)), gt_spec],
        out_specs=pl.BlockSpec((tm, d), lambda i: (i, 0)),
        scratch_shapes=[pltpu.VMEM((GATHER_SLOTS, TOP_K * tm * SUBLANES, LANES), F32),
                        pltpu.SemaphoreType.DMA((GATHER_SLOTS,))],
        compiler_params=_cparams(1, VMEM_LIMIT),
        name="moe_combine",
    )(pos_tiles, pos_tiles, pos_tiles, y_tiles, route, xall, g_post, mod5)


def _route_plan(route, tm, n_experts, tiles_per_b, latent_only):
    t = route.shape[0]
    e_tk = route[:, :TOP_K].astype(jnp.int32)
    n_routed = t
    if latent_only:
        is_ctx = (jnp.arange(t, dtype=jnp.int32) // tm) % tiles_per_b == 0
        e_tk = jnp.where(is_ctx[:, None], n_experts, e_tk)
        n_routed = t // tiles_per_b * (tiles_per_b - 1)
    n_tiles = (TOP_K * n_routed) // tm + n_experts
    e_flat = e_tk.T.reshape(-1)
    onehot = (e_flat[:, None] == jnp.arange(n_experts, dtype=jnp.int32)[None, :]).astype(jnp.int32)
    csum = jnp.cumsum(onehot, axis=0)
    counts = csum[-1]
    rank = jnp.sum(csum * onehot, axis=1) - 1
    group_tiles = (counts + tm - 1) // tm
    group_end = jnp.cumsum(group_tiles)
    group_start = (group_end - group_tiles) * tm
    pos = jnp.sum(onehot * group_start[None, :], axis=1) + rank
    tile_group = jnp.sum((jnp.arange(n_tiles, dtype=jnp.int32)[:, None] >= group_end[None, :]).astype(jnp.int32), axis=1)
    tile_valid = (tile_group < n_experts).astype(jnp.int32)
    tile_expert = jnp.minimum(tile_group, n_experts - 1)
    token = jnp.tile(jnp.arange(t, dtype=jnp.int32), TOP_K)
    routed = e_flat < n_experts
    scatter_row = jnp.where(routed, pos, n_tiles * tm + jnp.arange(TOP_K * t, dtype=jnp.int32))
    row_token = jnp.zeros((n_tiles * tm,), jnp.int32).at[scatter_row].set(token, unique_indices=True, mode="drop")
    pos = jnp.maximum(pos, 0)
    pos_tiles = pos.reshape(TOP_K, t // tm, tm).transpose(1, 0, 2).reshape(t // tm, 1, TOP_K * tm)
    return tile_expert, tile_valid, row_token, pos_tiles


def _rope_tables(seq, ctx_len):
    rows_n = seq // GRID_W
    row = jnp.repeat(jnp.arange(rows_n), GRID_W).astype(F32)
    col = jnp.tile(jnp.arange(GRID_W), rows_n).astype(F32)
    n_freq = QK_DIM // 4
    inv = ROPE_THETA ** (-jnp.arange(n_freq, dtype=F32) / n_freq)
    ang = jnp.concatenate([row[:, None] * inv, row[:, None] * inv, col[:, None] * inv, col[:, None] * inv], axis=1)
    cos, sin = jnp.cos(ang), jnp.sin(ang)
    first = (jnp.arange(QK_DIM) % (2 * n_freq)) < n_freq
    s1 = jnp.where(first[None, :], -sin, 0.0)
    s2 = jnp.where(first[None, :], 0.0, sin)
    pad = lambda a, v: jnp.concatenate([jnp.full((ctx_len, LANES), v, F32), jnp.tile(a, (1, 2))], axis=0)
    return pad(cos, 1.0), pad(s1, 0.0), pad(s2, 0.0)


def kernel(x, c, ctx, c_ctx, w_mod, b_mod, g_pre_mix, g_post_mix, g_pre_ffn, g_post_ffn, w_in, gla_gate_w2, gla_gate_b, diff_lambda, diff_subln, gla_onorm, w_out, ffn_w1, ffn_w3, ffn_w2, router, moe_w1, moe_w3, moe_w2):
    batch, seq, d = x.shape
    ctx_len = ctx.shape[1]
    depth = w_in.shape[0]
    n_experts = router.shape[-1]
    assert seq % ctx_len == 0 and ctx_len % GLA_CHUNK == 0 and d == 2 * HEAD_W == SUBLANES * LANES
    assert w_in.shape[-1] == _OFF_GLR + 2 * GATE_RANK
    tm = ctx_len
    tiles_per_b = (ctx_len + seq) // tm
    mod_rows = -(-(batch + 1) // SUBLANES) * SUBLANES
    geom = (tm, tiles_per_b, batch)

    cc = jnp.concatenate([c, c_ctx[None, :], jnp.zeros((mod_rows - batch - 1, d), F32)], axis=0)
    mod = _modulation(cc, w_mod, b_mod)
    mod5 = mod.reshape(depth, N_MOD, mod_rows, 1, d)
    rope_tabs = _rope_tables(seq, ctx_len)
    xall = jnp.concatenate([ctx, x], axis=1).reshape(batch * (ctx_len + seq), d)
    row2 = lambda a: a.reshape(1, -1)

    for l in range(depth):
        last = l == depth - 1
        lam_init = 0.8 - 0.6 * math.exp(-0.3 * l)
        w_main = w_in[l, :, :_OFF_GLR].astype(BF16)
        w_glr = jnp.pad(w_in[l, :, _OFF_GLR:], ((0, 0), (0, LANES - 2 * GATE_RANK))).astype(BF16)
        w2bd = jnp.zeros((LANES, 2 * GLA_KW), F32)
        w2bd = w2bd.at[:GATE_RANK, :GLA_KW].set(gla_gate_w2[l, 0])
        w2bd = w2bd.at[GATE_RANK:2 * GATE_RANK, GLA_KW:].set(gla_gate_w2[l, 1])
        w2bd = jnp.stack(_split2(w2bd))
        gate_b = gla_gate_b[l].reshape(1, 2 * GLA_KW)

        dq, dk, dv, gq, gk, gv, gg, gf, gb = _inproj(
            xall, row2(g_pre_mix[l]), mod5, l, w_main, w_glr, w2bd, gate_b, rope_tabs, geom)
        od = _diff_attn(dq, dk, dv, diff_lambda[l], row2(diff_subln[l]), lam_init, not last, geom, batch)
        og = _gla(gq, gk, gv, gf, gb, geom, batch)

        moe = l % 2 == 1
        idx = l // 2
        router_pad = None
        if moe:
            router_pad = (jnp.pad(router[idx], ((0, 0), (0, LANES - n_experts))), n_experts)
        outs = _merge(od, og, gg, xall, w_out[l].astype(BF16), row2(gla_onorm[l]), row2(g_post_mix[l]),
                      row2(g_pre_ffn[l]), mod5, l, router_pad, geom)
        if moe:
            xall, h_tiles, route = outs
            tile_expert, tile_valid, row_token, pos_tiles = _route_plan(route, tm, n_experts, tiles_per_b, last)
            y_tiles = _moe_ffn(h_tiles, tile_expert, tile_valid, row_token, moe_w1[idx].astype(BF16),
                               moe_w3[idx].astype(BF16), moe_w2[idx].astype(BF16), tm)
            xall = _moe_combine(y_tiles, pos_tiles, route, xall, row2(g_post_ffn[l]), mod5, l, geom, last)
        else:
            xall, h = outs
            xall = _dense_ffn(h, xall, ffn_w1[idx].astype(BF16), ffn_w3[idx].astype(BF16),
                              ffn_w2[idx].astype(BF16), row2(g_post_ffn[l]), mod5, l, geom)

    if xall.shape[0] == batch * seq:
        return xall.reshape(batch, seq, d)
    return xall.reshape(batch, ctx_len + seq, d)[:, ctx_len:, :]
```

```python
import functools
import math

import jax
import jax.numpy as jnp
from jax import lax
from jax.experimental import pallas as pl
from jax.experimental.pallas import tpu as pltpu

F32 = jnp.float32
BF16 = jnp.bfloat16
HIGHEST = lax.Precision.HIGHEST

N_HEADS = 4
QK_DIM = 64
V_DIM = 128
HEAD_W = N_HEADS * V_DIM
GLA_KW = N_HEADS * QK_DIM
GATE_RANK = 16
GATE_NORMALIZER = 16.0
GLA_CHUNK = 64
GRID_W = 64
ROPE_THETA = 10000.0
N_MOD = 6
TOP_K = 2
EPS = 1e-6
LOG2_E = 1.4426950408889634
LANES = 128
SUBLANES = 8
MXU_DIM = 256
VMEM_LIMIT = 56 * 1024 * 1024
DMA_ISSUE_UNROLL = 8

_OFF_DQ, _OFF_DK, _OFF_DV = 0, 512, 1024
_OFF_GQ, _OFF_GK, _OFF_GV, _OFF_GG, _OFF_GLR = 1536, 1792, 2048, 2560, 3072


def _rms(x, g):
    ms = jnp.mean(x * x, axis=-1, keepdims=True)
    return x * lax.rsqrt(ms + EPS) * g


def _silu(x):
    return x * (1.0 / (1.0 + jnp.exp(-x)))


def _split2(a):
    hi = a.astype(BF16)
    return hi, (a - hi.astype(F32)).astype(BF16)


def _cparams(n_axes, vmem=None):
    return pltpu.CompilerParams(dimension_semantics=("arbitrary",) * n_axes,
                                vmem_limit_bytes=vmem)


def _mod_row(i, tiles_per_b, ctx_row):
    return jnp.where(i % tiles_per_b == 0, ctx_row, i // tiles_per_b)


def _to_tiles(ref, val, row0=0):
    rows = val.shape[0]
    for c in range(SUBLANES):
        ref[pl.ds(row0 * SUBLANES + c, rows, stride=SUBLANES), :] = val[:, LANES * c:LANES * (c + 1)]


def _from_tiles(ref, rows, row0=0):
    return jnp.concatenate(
        [ref[pl.ds(row0 * SUBLANES + c, rows, stride=SUBLANES), :] for c in range(SUBLANES)], axis=1)


def _mod_kernel(c_ref, w_ref, b_ref, o_ref):
    o_ref[...] = jnp.dot(_silu(c_ref[...]), w_ref[...], precision=HIGHEST,
                         preferred_element_type=F32) + b_ref[...]


def _modulation(cc, w_mod, b_mod):
    n_layers, d, _ = w_mod.shape
    r = cc.shape[0]
    b4 = b_mod.reshape(n_layers, N_MOD, 1, d)
    return pl.pallas_call(
        _mod_kernel,
        out_shape=jax.ShapeDtypeStruct((n_layers, N_MOD, r, d), F32),
        grid=(n_layers, N_MOD),
        in_specs=[pl.BlockSpec((r, d), lambda l, j: (0, 0)),
                  pl.BlockSpec((None, d, d), lambda l, j: (l, 0, j)),
                  pl.BlockSpec((None, None, 1, d), lambda l, j: (l, j, 0, 0))],
        out_specs=pl.BlockSpec((None, None, r, d), lambda l, j: (l, j, 0, 0)),
        compiler_params=_cparams(2),
        name="modulation",
    )(cc, w_mod, b4)


def _inproj_kernel(x_ref, g_ref, sc_ref, sh_ref, w_ref, wglr_ref, w2_ref, gb_ref, tri_ref,
                   cos_ref, s1_ref, s2_ref,
                   dq_ref, dk_ref, dv_ref, gq_ref, gk_ref, gv_ref, gg_ref, gf_ref, gb_out_ref):
    h = _rms(x_ref[...], g_ref[...]) * (1.0 + sc_ref[...]) + sh_ref[...]
    hb = h.astype(BF16)
    cos, s1, s2 = cos_ref[...], s1_ref[...], s2_ref[...]

    def proj(off, n):
        return jnp.dot(hb, w_ref[:, off:off + n], preferred_element_type=F32)

    def rope(p):
        outs = []
        for j in range(N_HEADS):
            xg = p[:, LANES * j:LANES * (j + 1)]
            outs.append(xg * cos + pltpu.roll(xg, LANES - 16, 1) * s1 + pltpu.roll(xg, 16, 1) * s2)
        return jnp.concatenate(outs, axis=1)

    scale = QK_DIM ** -0.5
    dq_ref[...] = (rope(proj(_OFF_DQ, HEAD_W)) * (scale * LOG2_E)).astype(BF16)
    dk_ref[...] = rope(proj(_OFF_DK, HEAD_W)).astype(BF16)
    dv_ref[...] = proj(_OFF_DV, HEAD_W).astype(BF16)
    gq_ref[...] = proj(_OFF_GQ, GLA_KW) * scale
    gk_ref[...] = proj(_OFF_GK, GLA_KW)
    gv_ref[...] = proj(_OFF_GV, HEAD_W)
    gg_ref[...] = proj(_OFF_GG, HEAD_W)
    glr = jnp.dot(hb, wglr_ref[...], preferred_element_type=F32)
    glr_hi, glr_lo = _split2(glr)
    w2_hi, w2_lo = w2_ref[0], w2_ref[1]
    z = (jnp.dot(glr_hi, w2_hi, preferred_element_type=F32) + jnp.dot(glr_hi, w2_lo, preferred_element_type=F32)
         + jnp.dot(glr_lo, w2_hi, preferred_element_type=F32)) + gb_ref[...]
    gl = (jnp.minimum(z, 0.0) - jnp.log1p(jnp.exp(-jnp.abs(z)))) * (1.0 / GATE_NORMALIZER)

    def tri_sum(tri, a):
        hi, lo = _split2(a)
        return jnp.dot(tri, hi, preferred_element_type=F32) + jnp.dot(tri, lo, preferred_element_type=F32)

    gf_ref[...] = tri_sum(tri_ref[0], gl[:, :GLA_KW])
    gb_out_ref[...] = tri_sum(tri_ref[1], gl[:, GLA_KW:])


def _chunk_tri(tm):
    r = jnp.arange(tm)[:, None]
    c = jnp.arange(tm)[None, :]
    same = (r // GLA_CHUNK) == (c // GLA_CHUNK)
    return jnp.stack([same & (r >= c), same & (c >= r)]).astype(BF16)


def _inproj(xall, g_pre, mod5, layer, w_main, w_glr, w2bd, gate_b, rope_tabs, geom):
    t, d = xall.shape
    tm, tiles_per_b, ctx_row = geom
    n_tiles = t // tm
    tri = _chunk_tri(tm)

    def mod_spec(j):
        return pl.BlockSpec((None, None, None, 1, d),
                            lambda i: (layer, j, _mod_row(i, tiles_per_b, ctx_row), 0, 0))

    row = lambda w: pl.BlockSpec((tm, w), lambda i: (i, 0))
    full = lambda a: pl.BlockSpec(a.shape, lambda i: (0,) * a.ndim)
    tab = pl.BlockSpec((tm, LANES), lambda i: (i % tiles_per_b, 0))
    widths = (HEAD_W, HEAD_W, HEAD_W, GLA_KW, GLA_KW, HEAD_W, HEAD_W, GLA_KW, GLA_KW)
    dtypes = (BF16, BF16, BF16, F32, F32, F32, F32, F32, F32)
    return pl.pallas_call(
        _inproj_kernel,
        out_shape=[jax.ShapeDtypeStruct((t, w), dt) for w, dt in zip(widths, dtypes)],
        grid=(n_tiles,),
        in_specs=[row(d), full(g_pre), mod_spec(1), mod_spec(0), full(w_main), full(w_glr),
                  full(w2bd), full(gate_b), full(tri), tab, tab, tab],
        out_specs=[row(w) for w in widths],
        compiler_params=_cparams(1, VMEM_LIMIT),
        name="inproj",
    )(xall, g_pre, mod5, mod5, w_main, w_glr, w2bd, gate_b, tri, *rope_tabs)


ATTN_Q_TILES_PER_STEP = 2


def _diff_attn_kernel(q_ref, k_ref, v_ref, lam_ref, g_ref, o_ref, *, lam_init, q_tile):
    lv = lam_ref[...]
    lam = (jnp.exp(jnp.sum(lv[0:1] * lv[1:2], axis=-1, keepdims=True))
           - jnp.exp(jnp.sum(lv[2:3] * lv[3:4], axis=-1, keepdims=True)) + lam_init)
    lane = lax.broadcasted_iota(jnp.int32, (1, LANES), 1)
    ones = jnp.ones((k_ref.shape[0], V_DIM), BF16)
    for r0 in range(0, q_ref.shape[0], q_tile):
        rows = slice(r0, r0 + q_tile)
        for hh in range(N_HEADS):
            cols = slice(V_DIM * hh, V_DIM * (hh + 1))
            q = q_ref[rows, cols]
            k = k_ref[:, cols]
            v_ext = jnp.concatenate([v_ref[:, cols], ones], axis=1)

            def one(qm):
                s = lax.dot_general(qm, k, (((1,), (1,)), ((), ())), preferred_element_type=F32)
                p = jnp.exp2(s - jnp.max(s, axis=-1, keepdims=True))
                r = jnp.dot(p.astype(BF16), v_ext, preferred_element_type=F32)
                return r[:, :V_DIM] / r[:, V_DIM:V_DIM + 1]

            o = (one(jnp.where(lane < QK_DIM, q, jnp.zeros_like(q)))
                 - lam * one(jnp.where(lane >= QK_DIM, q, jnp.zeros_like(q))))
            o_ref[rows, cols] = (_rms(o, g_ref[...]) * (1.0 - lam_init)).astype(o_ref.dtype)


def _diff_attn(dq, dk, dv, lam_vec, subln_g, lam_init, with_ctx, geom, batch):
    tm, tiles_per_b, _ = geom
    seg = tm * tiles_per_b
    seq = seg - tm
    qrows = ATTN_Q_TILES_PER_STEP * tm
    assert seq % qrows == 0
    steps_per_b = seq // qrows
    dq3, dk3, dv3 = (a.reshape(batch, seg, HEAD_W) for a in (dq, dk, dv))
    small = [pl.BlockSpec(lam_vec.shape, lambda *_: (0, 0)), pl.BlockSpec(subln_g.shape, lambda *_: (0, 0))]
    kern = functools.partial(_diff_attn_kernel, lam_init=lam_init, q_tile=tm)
    od_lat = pl.pallas_call(
        kern,
        out_shape=jax.ShapeDtypeStruct((batch * seq, HEAD_W), BF16),
        grid=(batch, steps_per_b),
        in_specs=[pl.BlockSpec((pl.Element(qrows), pl.Element(HEAD_W)),
                               lambda b, j: (pl.multiple_of((b * tiles_per_b + 1 + ATTN_Q_TILES_PER_STEP * j) * tm,
                                                            tm), 0)),
                  pl.BlockSpec((None, seg, HEAD_W), lambda b, j: (b, 0, 0)),
                  pl.BlockSpec((None, seg, HEAD_W), lambda b, j: (b, 0, 0))] + small,
        out_specs=pl.BlockSpec((qrows, HEAD_W), lambda b, j: (b * steps_per_b + j, 0)),
        compiler_params=_cparams(2, VMEM_LIMIT),
        name="diff_attn",
    )(dq, dk3, dv3, lam_vec, subln_g)
    if not with_ctx:
        return od_lat, jnp.zeros((batch * tm, HEAD_W), BF16)
    ctx_spec = pl.BlockSpec((None, tm, HEAD_W), lambda b: (b, 0, 0))
    od_ctx = pl.pallas_call(
        kern,
        out_shape=jax.ShapeDtypeStruct((batch, tm, HEAD_W), BF16),
        grid=(batch,),
        in_specs=[ctx_spec, ctx_spec, ctx_spec] + small,
        out_specs=ctx_spec,
        compiler_params=_cparams(1, VMEM_LIMIT),
        name="diff_attn_ctx",
    )(dq3, dk3, dv3, lam_vec, subln_g)
    return od_lat, od_ctx.reshape(batch * tm, HEAD_W)


GLA_UNROLL = 4


def _gla_kernel(q_ref, k_ref, v_ref, gf_ref, gb_ref, o_ref, *, n_chunks, n_ctx_chunks):
    c = GLA_CHUNK
    r_i = lax.broadcasted_iota(jnp.int32, (N_HEADS * c, c), 0) % c
    c_i = lax.broadcasted_iota(jnp.int32, (N_HEADS * c, c), 1)
    lane = lax.broadcasted_iota(jnp.int32, (1, GLA_KW), 1)
    head_masks = [((lane // QK_DIM) == h).astype(F32) for h in range(N_HEADS)]
    causal = (r_i >= c_i, c_i >= r_i)
    last_row = (c - 1, 0)
    mid_row = (c // 2 - 1, c - c // 2)
    g_refs = (gf_ref, gb_ref)
    nt = (((1,), (1,)), ((), ()))
    tn = (((0,), (0,)), ((), ()))

    def head_stack(a):
        return jnp.concatenate([(a * head_masks[h]).astype(BF16) for h in range(N_HEADS)], axis=0)

    def chunk(direction, chunk_idx, st):
        rows = pl.ds(pl.multiple_of(chunk_idx * c, c), c)
        q, k, v, g = q_ref[rows, :], k_ref[rows, :], v_ref[rows, :], g_refs[direction][rows, :]
        last, mid = last_row[direction], mid_row[direction]
        g_last = g[last:last + 1, :]
        g_mid = g[mid:mid + 1, :]
        ke = (k * jnp.exp(g_mid - g)).astype(BF16)
        vb = v.astype(BF16)
        att = lax.dot_general(head_stack(q * jnp.exp(g - g_mid)), ke, nt, preferred_element_type=F32)
        att = jnp.where(causal[direction], att, 0.0).astype(BF16)
        inter = lax.dot_general(head_stack(q * jnp.exp(g)), st.astype(BF16), nt,
                                preferred_element_type=F32)
        o = jnp.concatenate(
            [jnp.dot(att[c * h:c * (h + 1), :], vb[:, V_DIM * h:V_DIM * (h + 1)], preferred_element_type=F32)
             + inter[c * h:c * (h + 1), :] for h in range(N_HEADS)], axis=1)
        o_ref[rows, :] = o_ref[rows, :] + o
        v_stack = jnp.concatenate([vb[:, V_DIM * h:V_DIM * (h + 1)] for h in range(N_HEADS)], axis=0)
        ds_t = lax.dot_general(v_stack, head_stack(k * jnp.exp(g_last - g)), tn,
                               preferred_element_type=F32)
        return st * jnp.exp(g_last) + ds_t

    o_ref[...] = jnp.zeros_like(o_ref)

    def body(i, carry):
        st_f, st_b = carry
        i_b = jnp.where(i < n_ctx_chunks, n_ctx_chunks - 1 - i, n_chunks - 1 - (i - n_ctx_chunks))
        return chunk(0, i, st_f), chunk(1, i_b, st_b)

    zero = jnp.zeros((V_DIM, GLA_KW), F32)
    lax.fori_loop(0, n_chunks, body, (zero, zero), unroll=GLA_UNROLL)


def _gla(gq, gk, gv, gf, gb, geom, batch):
    tm, tiles_per_b, _ = geom
    seg = tm * tiles_per_b
    r3 = lambda a: a.reshape(batch, seg, a.shape[-1])
    spec = lambda w: pl.BlockSpec((None, seg, w), lambda b: (b, 0, 0))
    kern = functools.partial(_gla_kernel, n_chunks=seg // GLA_CHUNK, n_ctx_chunks=tm // GLA_CHUNK)
    out = pl.pallas_call(
        kern,
        out_shape=jax.ShapeDtypeStruct((batch, seg, HEAD_W), F32),
        grid=(batch,),
        in_specs=[spec(GLA_KW), spec(GLA_KW), spec(HEAD_W), spec(GLA_KW), spec(GLA_KW)],
        out_specs=spec(HEAD_W),
        compiler_params=_cparams(1, VMEM_LIMIT),
        name="gla",
    )(r3(gq), r3(gk), r3(gv), r3(gf), r3(gb))
    return out.reshape(batch * seg, HEAD_W)


MERGE_ROW_SPLITS = 1


def _merge_kernel(od_lat_ref, od_ctx_ref, og_ref, gg_ref, x_ref, wout_ref, onorm_ref, gpost_ref, gt_ref,
                  gpre_ref, sc_ref, sh_ref, *rest, n_experts, tiles_per_b):
    if n_experts:
        router_ref, xo_ref, h_ref, route_ref = rest
    else:
        xo_ref, h_ref = rest
    onorm = onorm_ref[...]
    is_ctx = pl.program_id(0) % tiles_per_b == 0
    n_rows = x_ref.shape[0] // MERGE_ROW_SPLITS
    for r0 in range(0, x_ref.shape[0], n_rows):
        rs = pl.ds(r0, n_rows)
        og = og_ref[rs, :]
        ogn = jnp.concatenate([_rms(og[:, V_DIM * h:V_DIM * (h + 1)], onorm) for h in range(N_HEADS)], axis=1)
        ogn = (ogn * _silu(gg_ref[rs, :])).astype(BF16)
        od = jnp.where(is_ctx, od_ctx_ref[rs, :], od_lat_ref[rs, :])
        y = (jnp.dot(od, wout_ref[0:HEAD_W, :], preferred_element_type=F32)
             + jnp.dot(ogn, wout_ref[HEAD_W:2 * HEAD_W, :], preferred_element_type=F32))
        x_new = x_ref[rs, :] + gt_ref[...] * _rms(y, gpost_ref[...])
        xo_ref[rs, :] = x_new
        h = _rms(x_new, gpre_ref[...]) * (1.0 + sc_ref[...]) + sh_ref[...]
        if not n_experts:
            h_ref[rs, :] = h.astype(h_ref.dtype)
            continue
        _to_tiles(h_ref, h, r0)
        h_hi, h_lo = _split2(h)
        r_hi, r_lo = _split2(router_ref[...])
        logits = (jnp.dot(h_hi, r_hi, preferred_element_type=F32) + jnp.dot(h_hi, r_lo, preferred_element_type=F32)
                  + jnp.dot(h_lo, r_hi, preferred_element_type=F32))
        lane = lax.broadcasted_iota(jnp.int32, logits.shape, 1)
        neg = jnp.float32(-jnp.inf)
        lg = jnp.where(lane < n_experts, logits, neg)
        m1 = jnp.max(lg, axis=-1, keepdims=True)
        i1 = jnp.min(jnp.where(lg == m1, lane, LANES), axis=-1, keepdims=True)
        lg2 = jnp.where(lane == i1, neg, lg)
        m2 = jnp.max(lg2, axis=-1, keepdims=True)
        i2 = jnp.min(jnp.where(lg2 == m2, lane, LANES), axis=-1, keepdims=True)
        e2 = jnp.exp(m2 - m1)
        w1 = 1.0 / (1.0 + e2)
        w2 = e2 / (1.0 + e2)
        route_ref[rs, :] = jnp.where(lane == 0, i1.astype(F32),
                                     jnp.where(lane == 1, i2.astype(F32),
                                               jnp.where(lane == 2, w1, jnp.where(lane == 3, w2, 0.0))))


def _merge(od_lat, od_ctx, og, gg, xall, w_out, onorm, g_post, g_pre_ffn, mod5, layer, router_pad, geom):
    t, d = xall.shape
    tm, tiles_per_b, ctx_row = geom
    n_experts = 0 if router_pad is None else router_pad[1]
    lat_tiles = tiles_per_b - 1

    def mod_spec(j):
        return pl.BlockSpec((None, None, None, 1, d),
                            lambda i: (layer, j, _mod_row(i, tiles_per_b, ctx_row), 0, 0))

    row = lambda w: pl.BlockSpec((tm, w), lambda i: (i, 0))
    full = lambda a: pl.BlockSpec(a.shape, lambda i: (0,) * a.ndim)
    od_lat_spec = pl.BlockSpec(
        (tm, HEAD_W), lambda i: ((i // tiles_per_b) * lat_tiles + jnp.maximum(i % tiles_per_b - 1, 0), 0))
    od_ctx_spec = pl.BlockSpec((tm, HEAD_W), lambda i: (i // tiles_per_b, 0))
    in_specs = [od_lat_spec, od_ctx_spec, row(HEAD_W), row(HEAD_W), row(d), full(w_out), full(onorm),
                full(g_post), mod_spec(2), full(g_pre_ffn), mod_spec(4), mod_spec(3)]
    args = [od_lat, od_ctx, og, gg, xall, w_out, onorm, g_post, mod5, g_pre_ffn, mod5, mod5]
    out_shape = [jax.ShapeDtypeStruct((t, d), F32)]
    out_specs = [row(d)]
    if n_experts:
        in_specs.append(full(router_pad[0]))
        args.append(router_pad[0])
        out_shape += [jax.ShapeDtypeStruct((t * SUBLANES, LANES), F32), jax.ShapeDtypeStruct((t, LANES), F32)]
        out_specs += [pl.BlockSpec((tm * SUBLANES, LANES), lambda i: (i, 0)), row(LANES)]
    else:
        out_shape.append(jax.ShapeDtypeStruct((t, d), BF16))
        out_specs.append(row(d))
    return pl.pallas_call(
        functools.partial(_merge_kernel, n_experts=n_experts, tiles_per_b=tiles_per_b),
        out_shape=out_shape,
        grid=(t // tm,),
        in_specs=in_specs,
        out_specs=out_specs,
        compiler_params=_cparams(1, VMEM_LIMIT),
        name="merge_outproj",
    )(*args)


def _swiglu(xb, w1_ref, w3_ref, w2_ref):
    d_ff = w1_ref.shape[-1]
    cut = -(-(d_ff // 2) // MXU_DIM) * MXU_DIM
    y = None
    for c0, c1 in ((0, cut), (cut, d_ff)):
        a = jnp.dot(xb, w1_ref[:, c0:c1], preferred_element_type=F32)
        b = jnp.dot(xb, w3_ref[:, c0:c1], preferred_element_type=F32)
        m = (_silu(a) * b).astype(BF16)
        part = jnp.dot(m, w2_ref[c0:c1, :], preferred_element_type=F32)
        y = part if y is None else y + part
    return y


def _dense_ffn_kernel(h_ref, x_ref, w1_ref, w3_ref, w2_ref, gpost_ref, gt_ref, o_ref):
    y = _swiglu(h_ref[...], w1_ref, w3_ref, w2_ref)
    o_ref[...] = x_ref[...] + gt_ref[...] * _rms(y, gpost_ref[...])


def _dense_ffn(h, xall, w1, w3, w2, g_post, mod5, layer, geom):
    t, d = xall.shape
    tm, tiles_per_b, ctx_row = geom
    row = lambda w: pl.BlockSpec((tm, w), lambda i: (i, 0))
    resident = lambda a: pl.BlockSpec(a.shape, lambda i: (0,) * a.ndim, pipeline_mode=pl.Buffered(1))
    gt_spec = pl.BlockSpec((None, None, None, 1, d),
                           lambda i: (layer, 5, _mod_row(i, tiles_per_b, ctx_row), 0, 0))
    return pl.pallas_call(
        _dense_ffn_kernel,
        out_shape=jax.ShapeDtypeStruct((t, d), F32),
        grid=(t // tm,),
        in_specs=[row(d), row(d), resident(w1), resident(w3), resident(w2),
                  pl.BlockSpec(g_post.shape, lambda i: (0, 0)), gt_spec],
        out_specs=row(d),
        compiler_params=_cparams(1, VMEM_LIMIT),
        name="dense_ffn",
    )(h, xall, w1, w3, w2, g_post, mod5)


def _tile_copy(src, src_tok, dst, dst_tok, sem):
    s0 = pl.multiple_of(src_tok * SUBLANES, SUBLANES)
    d0 = pl.multiple_of(dst_tok * SUBLANES, SUBLANES)
    return pltpu.make_async_copy(src.at[pl.ds(s0, SUBLANES), :], dst.at[pl.ds(d0, SUBLANES), :], sem)


GATHER_SLOTS = 3


def _gather_step(src_hbm, idx_refs, buf, sem, n_copies, compute, valid=None, skipped=None):
    idx_ref, idx_next_ref, idx_next2_ref = idx_refs
    i = pl.program_id(0)
    slot = lax.rem(i, GATHER_SLOTS)
    slot1 = lax.rem(i + 1, GATHER_SLOTS)
    slot2 = lax.rem(i + 2, GATHER_SLOTS)

    def start(ref, s, r):
        _tile_copy(src_hbm, ref[0, 0, r], buf.at[s], r, sem.at[s]).start()

    def issue_loop(ref, s):
        lax.fori_loop(0, n_copies, lambda r, c: (start(ref, s, r), c)[1], 0, unroll=DMA_ISSUE_UNROLL)

    def wait_slot(s):
        def wait(r, carry):
            _tile_copy(src_hbm, 0, buf.at[s], r, sem.at[s]).wait()
            return carry
        lax.fori_loop(0, n_copies, wait, 0, unroll=DMA_ISSUE_UNROLL)

    def compute_and_issue():
        compute(buf.at[slot])
        for r in range(n_copies):
            start(idx_next2_ref, slot2, r)

    @pl.when(i == 0)
    def _():
        issue_loop(idx_ref, 0)
        issue_loop(idx_next_ref, 1)

    wait_slot(slot)
    if valid is None:
        compute_and_issue()
    else:
        pl.when(valid)(compute_and_issue)

        @pl.when(jnp.logical_not(valid))
        def _():
            skipped()
            issue_loop(idx_next2_ref, slot2)

    @pl.when(i == pl.num_programs(0) - 1)
    def _():
        wait_slot(slot1)
        wait_slot(slot2)


def _moe_ffn_kernel(te_ref, tv_ref, rows_ref, rows1_ref, rows2_ref, h_hbm, w1_ref, w3_ref, w2_ref, y_ref,
                    xbuf, sem, *, tm):
    del te_ref

    def compute(x_tiles):
        _to_tiles(y_ref, _swiglu(_from_tiles(x_tiles, tm).astype(BF16), w1_ref, w3_ref, w2_ref))

    def skipped():
        y_ref[...] = jnp.zeros_like(y_ref)

    _gather_step(h_hbm, (rows_ref, rows1_ref, rows2_ref), xbuf, sem, tm, compute,
                 valid=tv_ref[pl.program_id(0)] == 1, skipped=skipped)


def _ahead_specs(block, n_steps, tile=lambda i: i):
    def spec(k):
        return pl.BlockSpec((1, 1, block), lambda i, *_: (tile(jnp.minimum(i + k, n_steps - 1)), 0, 0),
                            memory_space=pltpu.SMEM)
    return [spec(k) for k in range(GATHER_SLOTS)]


def _moe_ffn(h_tiles, tile_expert, tile_valid, row_token, w1, w3, w2, tm):
    n_tiles = tile_expert.shape[0]
    _, d, d_ff = w1.shape
    rows = row_token.reshape(n_tiles, 1, tm)
    grid_spec = pltpu.PrefetchScalarGridSpec(
        num_scalar_prefetch=2,
        grid=(n_tiles,),
        in_specs=_ahead_specs(tm, n_tiles) + [
            pl.BlockSpec(memory_space=pl.ANY),
            pl.BlockSpec((None, d, d_ff), lambda i, te, tv: (te[i], 0, 0)),
            pl.BlockSpec((None, d, d_ff), lambda i, te, tv: (te[i], 0, 0)),
            pl.BlockSpec((None, d_ff, d), lambda i, te, tv: (te[i], 0, 0))],
        out_specs=pl.BlockSpec((tm * SUBLANES, LANES), lambda i, te, tv: (i, 0)),
        scratch_shapes=[pltpu.VMEM((GATHER_SLOTS, tm * SUBLANES, LANES), F32),
                        pltpu.SemaphoreType.DMA((GATHER_SLOTS,))],
    )
    return pl.pallas_call(
        functools.partial(_moe_ffn_kernel, tm=tm),
        out_shape=jax.ShapeDtypeStruct((n_tiles * tm * SUBLANES, LANES), F32),
        grid_spec=grid_spec,
        compiler_params=_cparams(1, VMEM_LIMIT),
        name="moe_ffn",
    )(tile_expert, tile_valid, rows, rows, rows, h_tiles, w1, w3, w2)


def _moe_combine_kernel(pos_ref, pos1_ref, pos2_ref, y_hbm, route_ref, x_ref, gpost_ref, gt_ref, o_ref,
                        ybuf, sem):
    tm = x_ref.shape[0]

    def compute(y_tiles):
        route = route_ref[...]
        y = route[:, 2:3] * _from_tiles(y_tiles, tm) + route[:, 3:4] * _from_tiles(y_tiles, tm, tm)
        o_ref[...] = x_ref[...] + gt_ref[...] * _rms(y, gpost_ref[...])

    _gather_step(y_hbm, (pos_ref, pos1_ref, pos2_ref), ybuf, sem, TOP_K * tm, compute)


def _moe_combine(y_tiles, pos_tiles, route, xall, g_post, mod5, layer, geom, latent_only):
    t, d = xall.shape
    tm, tiles_per_b, ctx_row = geom
    if latent_only:
        lat_tiles = tiles_per_b - 1
        n_steps = (t // tm // tiles_per_b) * lat_tiles
        tile = lambda i: (i // lat_tiles) * tiles_per_b + 1 + i % lat_tiles
    else:
        n_steps = t // tm
        tile = lambda i: i
    row = lambda w: pl.BlockSpec((tm, w), lambda i: (tile(i), 0))
    gt_spec = pl.BlockSpec((None, None, None, 1, d),
                           lambda i: (layer, 5, _mod_row(tile(i), tiles_per_b, ctx_row), 0, 0))
    return pl.pallas_call(
        _moe_combine_kernel,
        out_shape=jax.ShapeDtypeStruct((n_steps * tm, d), F32),
        grid=(n_steps,),
        in_specs=_ahead_specs(TOP_K * tm, n_steps, tile) + [
            pl.BlockSpec(memory_space=pl.ANY),
            row(LANES), row(d), pl.BlockSpec(g_post.shape, lambda i: (0, 0)), gt_spec],
        out_specs=pl.BlockSpec((tm, d), lambda i: (i, 0)),
        scratch_shapes=[pltpu.VMEM((GATHER_SLOTS, TOP_K * tm * SUBLANES, LANES), F32),
                        pltpu.SemaphoreType.DMA((GATHER_SLOTS,))],
        compiler_params=_cparams(1, VMEM_LIMIT),
        name="moe_combine",
    )(pos_tiles, pos_tiles, pos_tiles, y_tiles, route, xall, g_post, mod5)


def _route_plan(route, tm, n_experts, tiles_per_b, latent_only):
    t = route.shape[0]
    e_tk = route[:, :TOP_K].astype(jnp.int32)
    n_routed = t
    if latent_only:
        is_ctx = (jnp.arange(t, dtype=jnp.int32) // tm) % tiles_per_b == 0
        e_tk = jnp.where(is_ctx[:, None], n_experts, e_tk)
        n_routed = t // tiles_per_b * (tiles_per_b - 1)
    n_tiles = (TOP_K * n_routed) // tm + n_experts
    e_flat = e_tk.T.reshape(-1)
    onehot = (e_flat[:, None] == jnp.arange(n_experts, dtype=jnp.int32)[None, :]).astype(jnp.int32)
    csum = jnp.cumsum(onehot, axis=0)
    counts = csum[-1]
    rank = jnp.sum(csum * onehot, axis=1) - 1
    group_tiles = (counts + tm - 1) // tm
    group_end = jnp.cumsum(group_tiles)
    group_start = (group_end - group_tiles) * tm
    pos = jnp.sum(onehot * group_start[None, :], axis=1) + rank
    tile_group = jnp.sum((jnp.arange(n_tiles, dtype=jnp.int32)[:, None] >= group_end[None, :]).astype(jnp.int32), axis=1)
    tile_valid = (tile_group < n_experts).astype(jnp.int32)
    tile_expert = jnp.minimum(tile_group, n_experts - 1)
    n_rows = n_tiles * tm
    n_assign = TOP_K * t
    token = jnp.tile(jnp.arange(t, dtype=jnp.int32), TOP_K)
    routed = e_flat < n_experts
    assign_key = jnp.where(routed, pos, n_rows + jnp.arange(n_assign, dtype=jnp.int32))
    fill = jnp.arange(n_rows - TOP_K * n_routed, dtype=jnp.int32)
    pad = group_tiles * tm - counts
    pad_end = jnp.cumsum(pad)
    open_ = (fill[:, None] < pad_end[None, :]).astype(jnp.int32)
    own = open_ - jnp.concatenate([jnp.zeros_like(open_[:, :1]), open_[:, :-1]], axis=1)
    base = group_start + counts - (pad_end - pad)
    fill_key = jnp.where(open_[:, -1] == 1, jnp.sum(own * base[None, :], axis=1),
                         group_end[-1] * tm - pad_end[-1]) + fill
    keys = jnp.concatenate([assign_key, fill_key])
    vals = jnp.concatenate([token, jnp.zeros_like(fill)])
    row_token = lax.sort((keys, vals), num_keys=1)[1][:n_rows]
    pos = jnp.maximum(pos, 0)
    pos_tiles = pos.reshape(TOP_K, t // tm, tm).transpose(1, 0, 2).reshape(t // tm, 1, TOP_K * tm)
    return tile_expert, tile_valid, row_token, pos_tiles


def _rope_tables(seq, ctx_len):
    rows_n = seq // GRID_W
    row = jnp.repeat(jnp.arange(rows_n), GRID_W).astype(F32)
    col = jnp.tile(jnp.arange(GRID_W), rows_n).astype(F32)
    n_freq = QK_DIM // 4
    inv = ROPE_THETA ** (-jnp.arange(n_freq, dtype=F32) / n_freq)
    ang = jnp.concatenate([row[:, None] * inv, row[:, None] * inv, col[:, None] * inv, col[:, None] * inv], axis=1)
    cos, sin = jnp.cos(ang), jnp.sin(ang)
    first = (jnp.arange(QK_DIM) % (2 * n_freq)) < n_freq
    s1 = jnp.where(first[None, :], -sin, 0.0)
    s2 = jnp.where(first[None, :], 0.0, sin)
    pad = lambda a, v: jnp.concatenate([jnp.full((ctx_len, LANES), v, F32), jnp.tile(a, (1, 2))], axis=0)
    return pad(cos, 1.0), pad(s1, 0.0), pad(s2, 0.0)


def kernel(x, c, ctx, c_ctx, w_mod, b_mod, g_pre_mix, g_post_mix, g_pre_ffn, g_post_ffn, w_in, gla_gate_w2, gla_gate_b, diff_lambda, diff_subln, gla_onorm, w_out, ffn_w1, ffn_w3, ffn_w2, router, moe_w1, moe_w3, moe_w2):
    batch, seq, d = x.shape
    ctx_len = ctx.shape[1]
    depth = w_in.shape[0]
    n_experts = router.shape[-1]
    assert seq % ctx_len == 0 and ctx_len % GLA_CHUNK == 0 and d == 2 * HEAD_W == SUBLANES * LANES
    assert w_in.shape[-1] == _OFF_GLR + 2 * GATE_RANK
    tm = ctx_len
    tiles_per_b = (ctx_len + seq) // tm
    mod_rows = -(-(batch + 1) // SUBLANES) * SUBLANES
    geom = (tm, tiles_per_b, batch)

    cc = jnp.concatenate([c, c_ctx[None, :], jnp.zeros((mod_rows - batch - 1, d), F32)], axis=0)
    mod = _modulation(cc, w_mod, b_mod)
    mod5 = mod.reshape(depth, N_MOD, mod_rows, 1, d)
    rope_tabs = _rope_tables(seq, ctx_len)
    xall = jnp.concatenate([ctx, x], axis=1).reshape(batch * (ctx_len + seq), d)
    row2 = lambda a: a.reshape(1, -1)

    for l in range(depth):
        last = l == depth - 1
        lam_init = 0.8 - 0.6 * math.exp(-0.3 * l)
        w_main = w_in[l, :, :_OFF_GLR].astype(BF16)
        w_glr = jnp.pad(w_in[l, :, _OFF_GLR:], ((0, 0), (0, LANES - 2 * GATE_RANK))).astype(BF16)
        w2bd = jnp.zeros((LANES, 2 * GLA_KW), F32)
        w2bd = w2bd.at[:GATE_RANK, :GLA_KW].set(gla_gate_w2[l, 0])
        w2bd = w2bd.at[GATE_RANK:2 * GATE_RANK, GLA_KW:].set(gla_gate_w2[l, 1])
        w2bd = jnp.stack(_split2(w2bd))
        gate_b = gla_gate_b[l].reshape(1, 2 * GLA_KW)

        dq, dk, dv, gq, gk, gv, gg, gf, gb = _inproj(
            xall, row2(g_pre_mix[l]), mod5, l, w_main, w_glr, w2bd, gate_b, rope_tabs, geom)
        od_lat, od_ctx = _diff_attn(dq, dk, dv, diff_lambda[l], row2(diff_subln[l]), lam_init, not last,
                                    geom, batch)
        og = _gla(gq, gk, gv, gf, gb, geom, batch)

        moe = l % 2 == 1
        idx = l // 2
        router_pad = None
        if moe:
            router_pad = (jnp.pad(router[idx], ((0, 0), (0, LANES - n_experts))), n_experts)
        outs = _merge(od_lat, od_ctx, og, gg, xall, w_out[l].astype(BF16), row2(gla_onorm[l]),
                      row2(g_post_mix[l]), row2(g_pre_ffn[l]), mod5, l, router_pad, geom)
        if moe:
            xall, h_tiles, route = outs
            tile_expert, tile_valid, row_token, pos_tiles = _route_plan(route, tm, n_experts, tiles_per_b, last)
            y_tiles = _moe_ffn(h_tiles, tile_expert, tile_valid, row_token, moe_w1[idx].astype(BF16),
                               moe_w3[idx].astype(BF16), moe_w2[idx].astype(BF16), tm)
            xall = _moe_combine(y_tiles, pos_tiles, route, xall, row2(g_post_ffn[l]), mod5, l, geom, last)
        else:
            xall, h = outs
            xall = _dense_ffn(h, xall, ffn_w1[idx].astype(BF16), ffn_w3[idx].astype(BF16),
                              ffn_w2[idx].astype(BF16), row2(g_post_ffn[l]), mod5, l, geom)

    if xall.shape[0] == batch * seq:
        return xall.reshape(batch, seq, d)
    return xall.reshape(batch, ctx_len + seq, d)[:, ctx_len:, :]
```

```python
import functools
import math

import jax
import jax.numpy as jnp
from jax import lax
from jax.experimental import pallas as pl
from jax.experimental.pallas import tpu as pltpu

F32 = jnp.float32
BF16 = jnp.bfloat16
HIGHEST = lax.Precision.HIGHEST

N_HEADS = 4
QK_DIM = 64
V_DIM = 128
HEAD_W = N_HEADS * V_DIM
GLA_KW = N_HEADS * QK_DIM
GATE_RANK = 16
GATE_NORMALIZER = 16.0
GLA_CHUNK = 64
GRID_W = 64
ROPE_THETA = 10000.0
N_MOD = 6
TOP_K = 2
EPS = 1e-6
LOG2_E = 1.4426950408889634
LANES = 128
SUBLANES = 8
MXU_DIM = 256
VMEM_LIMIT = 56 * 1024 * 1024
DMA_ISSUE_UNROLL = 8

_OFF_DQ, _OFF_DK, _OFF_DV = 0, 512, 1024
_OFF_GQ, _OFF_GK, _OFF_GV, _OFF_GG, _OFF_GLR = 1536, 1792, 2048, 2560, 3072


def _rms(x, g):
    ms = jnp.mean(x * x, axis=-1, keepdims=True)
    return x * lax.rsqrt(ms + EPS) * g


def _silu(x):
    return x * (1.0 / (1.0 + jnp.exp(-x)))


def _split2(a):
    hi = a.astype(BF16)
    return hi, (a - hi.astype(F32)).astype(BF16)


def _cparams(n_axes, vmem=None):
    return pltpu.CompilerParams(dimension_semantics=("arbitrary",) * n_axes,
                                vmem_limit_bytes=vmem)


def _mod_row(i, tiles_per_b, ctx_row):
    return jnp.where(i % tiles_per_b == 0, ctx_row, i // tiles_per_b)


def _to_tiles(ref, val, row0=0):
    rows = val.shape[0]
    for c in range(SUBLANES):
        ref[pl.ds(row0 * SUBLANES + c, rows, stride=SUBLANES), :] = val[:, LANES * c:LANES * (c + 1)]


def _from_tiles(ref, rows, row0=0):
    return jnp.concatenate(
        [ref[pl.ds(row0 * SUBLANES + c, rows, stride=SUBLANES), :] for c in range(SUBLANES)], axis=1)


def _mod_kernel(c_ref, w_ref, b_ref, o_ref):
    o_ref[...] = jnp.dot(_silu(c_ref[...]), w_ref[...], precision=HIGHEST,
                         preferred_element_type=F32) + b_ref[...]


def _modulation(cc, w_mod, b_mod):
    n_layers, d, _ = w_mod.shape
    r = cc.shape[0]
    b4 = b_mod.reshape(n_layers, N_MOD, 1, d)
    return pl.pallas_call(
        _mod_kernel,
        out_shape=jax.ShapeDtypeStruct((n_layers, N_MOD, r, d), F32),
        grid=(n_layers, N_MOD),
        in_specs=[pl.BlockSpec((r, d), lambda l, j: (0, 0)),
                  pl.BlockSpec((None, d, d), lambda l, j: (l, 0, j)),
                  pl.BlockSpec((None, None, 1, d), lambda l, j: (l, j, 0, 0))],
        out_specs=pl.BlockSpec((None, None, r, d), lambda l, j: (l, j, 0, 0)),
        compiler_params=_cparams(2),
        name="modulation",
    )(cc, w_mod, b4)


def _inproj_kernel(x_ref, g_ref, sc_ref, sh_ref, w_ref, wglr_ref, w2_ref, gb_ref, tri_ref,
                   cos_ref, s1_ref, s2_ref,
                   dq_ref, dk_ref, dv_ref, gq_ref, gk_ref, gv_ref, gg_ref, gf_ref, gb_out_ref):
    h = _rms(x_ref[...], g_ref[...]) * (1.0 + sc_ref[...]) + sh_ref[...]
    hb = h.astype(BF16)
    cos, s1, s2 = cos_ref[...], s1_ref[...], s2_ref[...]

    def proj(off, n):
        return jnp.dot(hb, w_ref[:, off:off + n], preferred_element_type=F32)

    def rope(p):
        outs = []
        for j in range(N_HEADS):
            xg = p[:, LANES * j:LANES * (j + 1)]
            outs.append(xg * cos + pltpu.roll(xg, LANES - 16, 1) * s1 + pltpu.roll(xg, 16, 1) * s2)
        return jnp.concatenate(outs, axis=1)

    scale = QK_DIM ** -0.5
    dq_ref[...] = (rope(proj(_OFF_DQ, HEAD_W)) * (scale * LOG2_E)).astype(BF16)
    dk_ref[...] = rope(proj(_OFF_DK, HEAD_W)).astype(BF16)
    dv_ref[...] = proj(_OFF_DV, HEAD_W).astype(BF16)
    gq_ref[...] = proj(_OFF_GQ, GLA_KW) * scale
    gk_ref[...] = proj(_OFF_GK, GLA_KW)
    gv_ref[...] = proj(_OFF_GV, HEAD_W)
    gg_ref[...] = proj(_OFF_GG, HEAD_W)
    glr = jnp.dot(hb, wglr_ref[...], preferred_element_type=F32)
    glr_hi, glr_lo = _split2(glr)
    w2_hi, w2_lo = w2_ref[0], w2_ref[1]
    z = (jnp.dot(glr_hi, w2_hi, preferred_element_type=F32) + jnp.dot(glr_hi, w2_lo, preferred_element_type=F32)
         + jnp.dot(glr_lo, w2_hi, preferred_element_type=F32)) + gb_ref[...]
    gl = (jnp.minimum(z, 0.0) - jnp.log1p(jnp.exp(-jnp.abs(z)))) * (1.0 / GATE_NORMALIZER)

    def tri_sum(tri, a):
        hi, lo = _split2(a)
        return jnp.dot(tri, hi, preferred_element_type=F32) + jnp.dot(tri, lo, preferred_element_type=F32)

    gf_ref[...] = tri_sum(tri_ref[0], gl[:, :GLA_KW])
    gb_out_ref[...] = tri_sum(tri_ref[1], gl[:, GLA_KW:])


def _chunk_tri(tm):
    r = jnp.arange(tm)[:, None]
    c = jnp.arange(tm)[None, :]
    same = (r // GLA_CHUNK) == (c // GLA_CHUNK)
    return jnp.stack([same & (r >= c), same & (c >= r)]).astype(BF16)


def _inproj(xall, g_pre, mod5, layer, w_main, w_glr, w2bd, gate_b, rope_tabs, geom):
    t, d = xall.shape
    tm, tiles_per_b, ctx_row = geom
    n_tiles = t // tm
    tri = _chunk_tri(tm)

    def mod_spec(j):
        return pl.BlockSpec((None, None, None, 1, d),
                            lambda i: (layer, j, _mod_row(i, tiles_per_b, ctx_row), 0, 0))

    row = lambda w: pl.BlockSpec((tm, w), lambda i: (i, 0))
    full = lambda a: pl.BlockSpec(a.shape, lambda i: (0,) * a.ndim)
    tab = pl.BlockSpec((tm, LANES), lambda i: (i % tiles_per_b, 0))
    widths = (HEAD_W, HEAD_W, HEAD_W, GLA_KW, GLA_KW, HEAD_W, HEAD_W, GLA_KW, GLA_KW)
    dtypes = (BF16, BF16, BF16, F32, F32, F32, F32, F32, F32)
    return pl.pallas_call(
        _inproj_kernel,
        out_shape=[jax.ShapeDtypeStruct((t, w), dt) for w, dt in zip(widths, dtypes)],
        grid=(n_tiles,),
        in_specs=[row(d), full(g_pre), mod_spec(1), mod_spec(0), full(w_main), full(w_glr),
                  full(w2bd), full(gate_b), full(tri), tab, tab, tab],
        out_specs=[row(w) for w in widths],
        compiler_params=_cparams(1, VMEM_LIMIT),
        name="inproj",
    )(xall, g_pre, mod5, mod5, w_main, w_glr, w2bd, gate_b, tri, *rope_tabs)


ATTN_Q_TILES_PER_STEP = 4


def _diff_attn_kernel(q_ref, k_ref, v_ref, lam_ref, g_ref, o_ref, *, lam_init, q_tile):
    lv = lam_ref[...]
    lam = (jnp.exp(jnp.sum(lv[0:1] * lv[1:2], axis=-1, keepdims=True))
           - jnp.exp(jnp.sum(lv[2:3] * lv[3:4], axis=-1, keepdims=True)) + lam_init)
    lane = lax.broadcasted_iota(jnp.int32, (1, LANES), 1)
    ones = jnp.ones((k_ref.shape[0], V_DIM), BF16)
    for r0 in range(0, q_ref.shape[0], q_tile):
        rows = slice(r0, r0 + q_tile)
        for hh in range(N_HEADS):
            cols = slice(V_DIM * hh, V_DIM * (hh + 1))
            q = q_ref[rows, cols]
            k = k_ref[:, cols]
            v_ext = jnp.concatenate([v_ref[:, cols], ones], axis=1)

            def one(qm):
                s = lax.dot_general(qm, k, (((1,), (1,)), ((), ())), preferred_element_type=F32)
                p = jnp.exp2(s - jnp.max(s, axis=-1, keepdims=True))
                r = jnp.dot(p.astype(BF16), v_ext, preferred_element_type=F32)
                return r[:, :V_DIM] / r[:, V_DIM:V_DIM + 1]

            o = (one(jnp.where(lane < QK_DIM, q, jnp.zeros_like(q)))
                 - lam * one(jnp.where(lane >= QK_DIM, q, jnp.zeros_like(q))))
            o_ref[rows, cols] = (_rms(o, g_ref[...]) * (1.0 - lam_init)).astype(o_ref.dtype)


def _diff_attn(dq, dk, dv, lam_vec, subln_g, lam_init, with_ctx, geom, batch):
    tm, tiles_per_b, _ = geom
    seg = tm * tiles_per_b
    seq = seg - tm
    q_tiles = math.gcd(ATTN_Q_TILES_PER_STEP, seq // tm)
    qrows = q_tiles * tm
    steps_per_b = seq // qrows
    dq3, dk3, dv3 = (a.reshape(batch, seg, HEAD_W) for a in (dq, dk, dv))
    small = [pl.BlockSpec(lam_vec.shape, lambda *_: (0, 0)), pl.BlockSpec(subln_g.shape, lambda *_: (0, 0))]
    kern = functools.partial(_diff_attn_kernel, lam_init=lam_init, q_tile=tm)
    od_lat = pl.pallas_call(
        kern,
        out_shape=jax.ShapeDtypeStruct((batch * seq, HEAD_W), BF16),
        grid=(batch, steps_per_b),
        in_specs=[pl.BlockSpec((pl.Element(qrows), pl.Element(HEAD_W)),
                               lambda b, j: (pl.multiple_of((b * tiles_per_b + 1 + q_tiles * j) * tm, tm), 0)),
                  pl.BlockSpec((None, seg, HEAD_W), lambda b, j: (b, 0, 0)),
                  pl.BlockSpec((None, seg, HEAD_W), lambda b, j: (b, 0, 0))] + small,
        out_specs=pl.BlockSpec((qrows, HEAD_W), lambda b, j: (b * steps_per_b + j, 0)),
        compiler_params=_cparams(2, VMEM_LIMIT),
        name="diff_attn",
    )(dq, dk3, dv3, lam_vec, subln_g)
    if not with_ctx:
        return od_lat, jnp.zeros((batch * tm, HEAD_W), BF16)
    ctx_spec = pl.BlockSpec((None, tm, HEAD_W), lambda b: (b, 0, 0))
    od_ctx = pl.pallas_call(
        kern,
        out_shape=jax.ShapeDtypeStruct((batch, tm, HEAD_W), BF16),
        grid=(batch,),
        in_specs=[ctx_spec, ctx_spec, ctx_spec] + small,
        out_specs=ctx_spec,
        compiler_params=_cparams(1, VMEM_LIMIT),
        name="diff_attn_ctx",
    )(dq3, dk3, dv3, lam_vec, subln_g)
    return od_lat, od_ctx.reshape(batch * tm, HEAD_W)


GLA_UNROLL = 4


def _gla_kernel(q_ref, k_ref, v_ref, gf_ref, gb_ref, o_ref, *, n_chunks, n_ctx_chunks):
    c = GLA_CHUNK
    r_i = lax.broadcasted_iota(jnp.int32, (N_HEADS * c, c), 0) % c
    c_i = lax.broadcasted_iota(jnp.int32, (N_HEADS * c, c), 1)
    lane = lax.broadcasted_iota(jnp.int32, (1, GLA_KW), 1)
    head_masks = [((lane // QK_DIM) == h).astype(F32) for h in range(N_HEADS)]
    causal = (r_i >= c_i, c_i >= r_i)
    last_row = (c - 1, 0)
    mid_row = (c // 2 - 1, c - c // 2)
    g_refs = (gf_ref, gb_ref)
    nt = (((1,), (1,)), ((), ()))
    tn = (((0,), (0,)), ((), ()))

    def head_stack(a):
        return jnp.concatenate([(a * head_masks[h]).astype(BF16) for h in range(N_HEADS)], axis=0)

    def chunk(direction, chunk_idx, st):
        rows = pl.ds(pl.multiple_of(chunk_idx * c, c), c)
        q, k, v, g = q_ref[rows, :], k_ref[rows, :], v_ref[rows, :], g_refs[direction][rows, :]
        last, mid = last_row[direction], mid_row[direction]
        g_last = g[last:last + 1, :]
        g_mid = g[mid:mid + 1, :]
        ke = (k * jnp.exp(g_mid - g)).astype(BF16)
        vb = v.astype(BF16)
        att = lax.dot_general(head_stack(q * jnp.exp(g - g_mid)), ke, nt, preferred_element_type=F32)
        att = jnp.where(causal[direction], att, 0.0).astype(BF16)
        inter = lax.dot_general(head_stack(q * jnp.exp(g)), st.astype(BF16), nt,
                                preferred_element_type=F32)
        o = jnp.concatenate(
            [jnp.dot(att[c * h:c * (h + 1), :], vb[:, V_DIM * h:V_DIM * (h + 1)], preferred_element_type=F32)
             + inter[c * h:c * (h + 1), :] for h in range(N_HEADS)], axis=1)
        o_ref[rows, :] = o_ref[rows, :] + o
        v_stack = jnp.concatenate([vb[:, V_DIM * h:V_DIM * (h + 1)] for h in range(N_HEADS)], axis=0)
        ds_t = lax.dot_general(v_stack, head_stack(k * jnp.exp(g_last - g)), tn,
                               preferred_element_type=F32)
        return st * jnp.exp(g_last) + ds_t

    o_ref[...] = jnp.zeros_like(o_ref)

    def body(i, carry):
        st_f, st_b = carry
        i_b = jnp.where(i < n_ctx_chunks, n_ctx_chunks - 1 - i, n_chunks - 1 - (i - n_ctx_chunks))
        return chunk(0, i, st_f), chunk(1, i_b, st_b)

    zero = jnp.zeros((V_DIM, GLA_KW), F32)
    lax.fori_loop(0, n_chunks, body, (zero, zero), unroll=GLA_UNROLL)


def _gla(gq, gk, gv, gf, gb, geom, batch):
    tm, tiles_per_b, _ = geom
    seg = tm * tiles_per_b
    r3 = lambda a: a.reshape(batch, seg, a.shape[-1])
    spec = lambda w: pl.BlockSpec((None, seg, w), lambda b: (b, 0, 0))
    kern = functools.partial(_gla_kernel, n_chunks=seg // GLA_CHUNK, n_ctx_chunks=tm // GLA_CHUNK)
    out = pl.pallas_call(
        kern,
        out_shape=jax.ShapeDtypeStruct((batch, seg, HEAD_W), F32),
        grid=(batch,),
        in_specs=[spec(GLA_KW), spec(GLA_KW), spec(HEAD_W), spec(GLA_KW), spec(GLA_KW)],
        out_specs=spec(HEAD_W),
        compiler_params=_cparams(1, VMEM_LIMIT),
        name="gla",
    )(r3(gq), r3(gk), r3(gv), r3(gf), r3(gb))
    return out.reshape(batch * seg, HEAD_W)


MERGE_ROW_SPLITS = 1


def _merge_kernel(od_lat_ref, od_ctx_ref, og_ref, gg_ref, x_ref, wout_ref, onorm_ref, gpost_ref, gt_ref,
                  gpre_ref, sc_ref, sh_ref, *rest, n_experts, tiles_per_b):
    if n_experts:
        router_ref, xo_ref, h_ref, route_ref = rest
    else:
        xo_ref, h_ref = rest
    onorm = onorm_ref[...]
    is_ctx = pl.program_id(0) % tiles_per_b == 0
    n_rows = x_ref.shape[0] // MERGE_ROW_SPLITS
    for r0 in range(0, x_ref.shape[0], n_rows):
        rs = pl.ds(r0, n_rows)
        og = og_ref[rs, :]
        ogn = jnp.concatenate([_rms(og[:, V_DIM * h:V_DIM * (h + 1)], onorm) for h in range(N_HEADS)], axis=1)
        ogn = (ogn * _silu(gg_ref[rs, :])).astype(BF16)
        od = jnp.where(is_ctx, od_ctx_ref[rs, :], od_lat_ref[rs, :])
        y = (jnp.dot(od, wout_ref[0:HEAD_W, :], preferred_element_type=F32)
             + jnp.dot(ogn, wout_ref[HEAD_W:2 * HEAD_W, :], preferred_element_type=F32))
        x_new = x_ref[rs, :] + gt_ref[...] * _rms(y, gpost_ref[...])
        xo_ref[rs, :] = x_new
        h = _rms(x_new, gpre_ref[...]) * (1.0 + sc_ref[...]) + sh_ref[...]
        if not n_experts:
            h_ref[rs, :] = h.astype(h_ref.dtype)
            continue
        _to_tiles(h_ref, h, r0)
        h_hi, h_lo = _split2(h)
        r_hi, r_lo = _split2(router_ref[...])
        logits = (jnp.dot(h_hi, r_hi, preferred_element_type=F32) + jnp.dot(h_hi, r_lo, preferred_element_type=F32)
                  + jnp.dot(h_lo, r_hi, preferred_element_type=F32))
        lane = lax.broadcasted_iota(jnp.int32, logits.shape, 1)
        neg = jnp.float32(-jnp.inf)
        lg = jnp.where(lane < n_experts, logits, neg)
        m1 = jnp.max(lg, axis=-1, keepdims=True)
        i1 = jnp.min(jnp.where(lg == m1, lane, LANES), axis=-1, keepdims=True)
        lg2 = jnp.where(lane == i1, neg, lg)
        m2 = jnp.max(lg2, axis=-1, keepdims=True)
        i2 = jnp.min(jnp.where(lg2 == m2, lane, LANES), axis=-1, keepdims=True)
        e2 = jnp.exp(m2 - m1)
        w1 = 1.0 / (1.0 + e2)
        w2 = e2 / (1.0 + e2)
        route_ref[rs, :] = jnp.where(lane == 0, i1.astype(F32),
                                     jnp.where(lane == 1, i2.astype(F32),
                                               jnp.where(lane == 2, w1, jnp.where(lane == 3, w2, 0.0))))


def _merge(od_lat, od_ctx, og, gg, xall, w_out, onorm, g_post, g_pre_ffn, mod5, layer, router_pad, geom):
    t, d = xall.shape
    tm, tiles_per_b, ctx_row = geom
    n_experts = 0 if router_pad is None else router_pad[1]
    lat_tiles = tiles_per_b - 1

    def mod_spec(j):
        return pl.BlockSpec((None, None, None, 1, d),
                            lambda i: (layer, j, _mod_row(i, tiles_per_b, ctx_row), 0, 0))

    row = lambda w: pl.BlockSpec((tm, w), lambda i: (i, 0))
    full = lambda a: pl.BlockSpec(a.shape, lambda i: (0,) * a.ndim)
    od_lat_spec = pl.BlockSpec(
        (tm, HEAD_W), lambda i: ((i // tiles_per_b) * lat_tiles + jnp.maximum(i % tiles_per_b - 1, 0), 0))
    od_ctx_spec = pl.BlockSpec((tm, HEAD_W), lambda i: (i // tiles_per_b, 0))
    in_specs = [od_lat_spec, od_ctx_spec, row(HEAD_W), row(HEAD_W), row(d), full(w_out), full(onorm),
                full(g_post), mod_spec(2), full(g_pre_ffn), mod_spec(4), mod_spec(3)]
    args = [od_lat, od_ctx, og, gg, xall, w_out, onorm, g_post, mod5, g_pre_ffn, mod5, mod5]
    out_shape = [jax.ShapeDtypeStruct((t, d), F32)]
    out_specs = [row(d)]
    if n_experts:
        in_specs.append(full(router_pad[0]))
        args.append(router_pad[0])
        out_shape += [jax.ShapeDtypeStruct((t * SUBLANES, LANES), F32), jax.ShapeDtypeStruct((t, LANES), F32)]
        out_specs += [pl.BlockSpec((tm * SUBLANES, LANES), lambda i: (i, 0)), row(LANES)]
    else:
        out_shape.append(jax.ShapeDtypeStruct((t, d), BF16))
        out_specs.append(row(d))
    return pl.pallas_call(
        functools.partial(_merge_kernel, n_experts=n_experts, tiles_per_b=tiles_per_b),
        out_shape=out_shape,
        grid=(t // tm,),
        in_specs=in_specs,
        out_specs=out_specs,
        compiler_params=_cparams(1, VMEM_LIMIT),
        name="merge_outproj",
    )(*args)


def _swiglu(xb, w1_ref, w3_ref, w2_ref):
    d_ff = w1_ref.shape[-1]
    cut = -(-(d_ff // 2) // MXU_DIM) * MXU_DIM
    y = None
    for c0, c1 in ((0, cut), (cut, d_ff)):
        a = jnp.dot(xb, w1_ref[:, c0:c1], preferred_element_type=F32)
        b = jnp.dot(xb, w3_ref[:, c0:c1], preferred_element_type=F32)
        m = (_silu(a) * b).astype(BF16)
        part = jnp.dot(m, w2_ref[c0:c1, :], preferred_element_type=F32)
        y = part if y is None else y + part
    return y


def _dense_ffn_kernel(h_ref, x_ref, w1_ref, w3_ref, w2_ref, gpost_ref, gt_ref, o_ref):
    y = _swiglu(h_ref[...], w1_ref, w3_ref, w2_ref)
    o_ref[...] = x_ref[...] + gt_ref[...] * _rms(y, gpost_ref[...])


def _dense_ffn(h, xall, w1, w3, w2, g_post, mod5, layer, geom):
    t, d = xall.shape
    tm, tiles_per_b, ctx_row = geom
    row = lambda w: pl.BlockSpec((tm, w), lambda i: (i, 0))
    resident = lambda a: pl.BlockSpec(a.shape, lambda i: (0,) * a.ndim, pipeline_mode=pl.Buffered(1))
    gt_spec = pl.BlockSpec((None, None, None, 1, d),
                           lambda i: (layer, 5, _mod_row(i, tiles_per_b, ctx_row), 0, 0))
    return pl.pallas_call(
        _dense_ffn_kernel,
        out_shape=jax.ShapeDtypeStruct((t, d), F32),
        grid=(t // tm,),
        in_specs=[row(d), row(d), resident(w1), resident(w3), resident(w2),
                  pl.BlockSpec(g_post.shape, lambda i: (0, 0)), gt_spec],
        out_specs=row(d),
        compiler_params=_cparams(1, VMEM_LIMIT),
        name="dense_ffn",
    )(h, xall, w1, w3, w2, g_post, mod5)


def _tile_copy(src, src_tok, dst, dst_tok, sem):
    s0 = pl.multiple_of(src_tok * SUBLANES, SUBLANES)
    d0 = pl.multiple_of(dst_tok * SUBLANES, SUBLANES)
    return pltpu.make_async_copy(src.at[pl.ds(s0, SUBLANES), :], dst.at[pl.ds(d0, SUBLANES), :], sem)


GATHER_SLOTS = 3


def _gather_step(src_hbm, idx_refs, buf, sem, n_copies, compute, valid=None, skipped=None):
    idx_ref, idx_next_ref, idx_next2_ref = idx_refs
    i = pl.program_id(0)
    slot = lax.rem(i, GATHER_SLOTS)
    slot1 = lax.rem(i + 1, GATHER_SLOTS)
    slot2 = lax.rem(i + 2, GATHER_SLOTS)

    def start(ref, s, r):
        _tile_copy(src_hbm, ref[0, 0, r], buf.at[s], r, sem.at[s]).start()

    def issue_loop(ref, s):
        lax.fori_loop(0, n_copies, lambda r, c: (start(ref, s, r), c)[1], 0, unroll=DMA_ISSUE_UNROLL)

    def wait_slot(s):
        def wait(r, carry):
            _tile_copy(src_hbm, 0, buf.at[s], r, sem.at[s]).wait()
            return carry
        lax.fori_loop(0, n_copies, wait, 0, unroll=DMA_ISSUE_UNROLL)

    def compute_and_issue():
        compute(buf.at[slot])
        for r in range(n_copies):
            start(idx_next2_ref, slot2, r)

    @pl.when(i == 0)
    def _():
        issue_loop(idx_ref, 0)
        issue_loop(idx_next_ref, 1)

    wait_slot(slot)
    if valid is None:
        compute_and_issue()
    else:
        pl.when(valid)(compute_and_issue)

        @pl.when(jnp.logical_not(valid))
        def _():
            skipped()
            issue_loop(idx_next2_ref, slot2)

    @pl.when(i == pl.num_programs(0) - 1)
    def _():
        wait_slot(slot1)
        wait_slot(slot2)


def _moe_ffn_kernel(te_ref, tv_ref, rows_ref, rows1_ref, rows2_ref, h_hbm, w1_ref, w3_ref, w2_ref, y_ref,
                    xbuf, sem, *, tm):
    del te_ref

    def compute(x_tiles):
        _to_tiles(y_ref, _swiglu(_from_tiles(x_tiles, tm).astype(BF16), w1_ref, w3_ref, w2_ref))

    def skipped():
        y_ref[...] = jnp.zeros_like(y_ref)

    _gather_step(h_hbm, (rows_ref, rows1_ref, rows2_ref), xbuf, sem, tm, compute,
                 valid=tv_ref[pl.program_id(0)] == 1, skipped=skipped)


def _ahead_specs(block, n_steps, tile=lambda i: i):
    def spec(k):
        return pl.BlockSpec((1, 1, block), lambda i, *_: (tile(jnp.minimum(i + k, n_steps - 1)), 0, 0),
                            memory_space=pltpu.SMEM)
    return [spec(k) for k in range(GATHER_SLOTS)]


def _moe_ffn(h_tiles, tile_expert, tile_valid, row_token, w1, w3, w2, tm):
    n_tiles = tile_expert.shape[0]
    _, d, d_ff = w1.shape
    rows = row_token.reshape(n_tiles, 1, tm)
    grid_spec = pltpu.PrefetchScalarGridSpec(
        num_scalar_prefetch=2,
        grid=(n_tiles,),
        in_specs=_ahead_specs(tm, n_tiles) + [
            pl.BlockSpec(memory_space=pl.ANY),
            pl.BlockSpec((None, d, d_ff), lambda i, te, tv: (te[i], 0, 0)),
            pl.BlockSpec((None, d, d_ff), lambda i, te, tv: (te[i], 0, 0)),
            pl.BlockSpec((None, d_ff, d), lambda i, te, tv: (te[i], 0, 0))],
        out_specs=pl.BlockSpec((tm * SUBLANES, LANES), lambda i, te, tv: (i, 0)),
        scratch_shapes=[pltpu.VMEM((GATHER_SLOTS, tm * SUBLANES, LANES), F32),
                        pltpu.SemaphoreType.DMA((GATHER_SLOTS,))],
    )
    return pl.pallas_call(
        functools.partial(_moe_ffn_kernel, tm=tm),
        out_shape=jax.ShapeDtypeStruct((n_tiles * tm * SUBLANES, LANES), F32),
        grid_spec=grid_spec,
        compiler_params=_cparams(1, VMEM_LIMIT),
        name="moe_ffn",
    )(tile_expert, tile_valid, rows, rows, rows, h_tiles, w1, w3, w2)


def _moe_combine_kernel(pos_ref, pos1_ref, pos2_ref, y_hbm, route_ref, x_ref, gpost_ref, gt_ref, o_ref,
                        ybuf, sem):
    tm = x_ref.shape[0]

    def compute(y_tiles):
        route = route_ref[...]
        y = route[:, 2:3] * _from_tiles(y_tiles, tm) + route[:, 3:4] * _from_tiles(y_tiles, tm, tm)
        o_ref[...] = x_ref[...] + gt_ref[...] * _rms(y, gpost_ref[...])

    _gather_step(y_hbm, (pos_ref, pos1_ref, pos2_ref), ybuf, sem, TOP_K * tm, compute)


def _moe_combine(y_tiles, pos_tiles, route, xall, g_post, mod5, layer, geom, latent_only):
    t, d = xall.shape
    tm, tiles_per_b, ctx_row = geom
    if latent_only:
        lat_tiles = tiles_per_b - 1
        n_steps = (t // tm // tiles_per_b) * lat_tiles
        tile = lambda i: (i // lat_tiles) * tiles_per_b + 1 + i % lat_tiles
    else:
        n_steps = t // tm
        tile = lambda i: i
    row = lambda w: pl.BlockSpec((tm, w), lambda i: (tile(i), 0))
    gt_spec = pl.BlockSpec((None, None, None, 1, d),
                           lambda i: (layer, 5, _mod_row(tile(i), tiles_per_b, ctx_row), 0, 0))
    return pl.pallas_call(
        _moe_combine_kernel,
        out_shape=jax.ShapeDtypeStruct((n_steps * tm, d), F32),
        grid=(n_steps,),
        in_specs=_ahead_specs(TOP_K * tm, n_steps, tile) + [
            pl.BlockSpec(memory_space=pl.ANY),
            row(LANES), row(d), pl.BlockSpec(g_post.shape, lambda i: (0, 0)), gt_spec],
        out_specs=pl.BlockSpec((tm, d), lambda i: (i, 0)),
        scratch_shapes=[pltpu.VMEM((GATHER_SLOTS, TOP_K * tm * SUBLANES, LANES), F32),
                        pltpu.SemaphoreType.DMA((GATHER_SLOTS,))],
        compiler_params=_cparams(1, VMEM_LIMIT),
        name="moe_combine",
    )(pos_tiles, pos_tiles, pos_tiles, y_tiles, route, xall, g_post, mod5)


def _route_plan(route, tm, n_experts, tiles_per_b, latent_only):
    t = route.shape[0]
    e_tk = route[:, :TOP_K].astype(jnp.int32)
    n_routed = t
    if latent_only:
        is_ctx = (jnp.arange(t, dtype=jnp.int32) // tm) % tiles_per_b == 0
        e_tk = jnp.where(is_ctx[:, None], n_experts, e_tk)
        n_routed = t // tiles_per_b * (tiles_per_b - 1)
    n_tiles = (TOP_K * n_routed) // tm + n_experts
    e_flat = e_tk.T.reshape(-1)
    onehot = (e_flat[:, None] == jnp.arange(n_experts, dtype=jnp.int32)[None, :]).astype(jnp.int32)
    csum = jnp.cumsum(onehot, axis=0)
    counts = csum[-1]
    rank = jnp.sum(csum * onehot, axis=1) - 1
    group_tiles = (counts + tm - 1) // tm
    group_end = jnp.cumsum(group_tiles)
    group_start = (group_end - group_tiles) * tm
    pos = jnp.sum(onehot * group_start[None, :], axis=1) + rank
    tile_group = jnp.sum((jnp.arange(n_tiles, dtype=jnp.int32)[:, None] >= group_end[None, :]).astype(jnp.int32), axis=1)
    tile_valid = (tile_group < n_experts).astype(jnp.int32)
    tile_expert = jnp.minimum(tile_group, n_experts - 1)
    n_rows = n_tiles * tm
    n_assign = TOP_K * t
    token = jnp.tile(jnp.arange(t, dtype=jnp.int32), TOP_K)
    routed = e_flat < n_experts
    assign_key = jnp.where(routed, pos, n_rows + jnp.arange(n_assign, dtype=jnp.int32))
    fill = jnp.arange(n_rows - TOP_K * n_routed, dtype=jnp.int32)
    pad = group_tiles * tm - counts
    pad_end = jnp.cumsum(pad)
    open_ = (fill[:, None] < pad_end[None, :]).astype(jnp.int32)
    own = open_ - jnp.concatenate([jnp.zeros_like(open_[:, :1]), open_[:, :-1]], axis=1)
    base = group_start + counts - (pad_end - pad)
    fill_key = jnp.where(open_[:, -1] == 1, jnp.sum(own * base[None, :], axis=1),
                         group_end[-1] * tm - pad_end[-1]) + fill
    keys = jnp.concatenate([assign_key, fill_key])
    vals = jnp.concatenate([token, jnp.zeros_like(fill)])
    row_token = lax.sort((keys, vals), num_keys=1)[1][:n_rows]
    pos = jnp.maximum(pos, 0)
    pos_tiles = pos.reshape(TOP_K, t // tm, tm).transpose(1, 0, 2).reshape(t // tm, 1, TOP_K * tm)
    return tile_expert, tile_valid, row_token, pos_tiles


def _rope_tables(seq, ctx_len):
    rows_n = seq // GRID_W
    row = jnp.repeat(jnp.arange(rows_n), GRID_W).astype(F32)
    col = jnp.tile(jnp.arange(GRID_W), rows_n).astype(F32)
    n_freq = QK_DIM // 4
    inv = ROPE_THETA ** (-jnp.arange(n_freq, dtype=F32) / n_freq)
    ang = jnp.concatenate([row[:, None] * inv, row[:, None] * inv, col[:, None] * inv, col[:, None] * inv], axis=1)
    cos, sin = jnp.cos(ang), jnp.sin(ang)
    first = (jnp.arange(QK_DIM) % (2 * n_freq)) < n_freq
    s1 = jnp.where(first[None, :], -sin, 0.0)
    s2 = jnp.where(first[None, :], 0.0, sin)
    pad = lambda a, v: jnp.concatenate([jnp.full((ctx_len, LANES), v, F32), jnp.tile(a, (1, 2))], axis=0)
    return pad(cos, 1.0), pad(s1, 0.0), pad(s2, 0.0)


def kernel(x, c, ctx, c_ctx, w_mod, b_mod, g_pre_mix, g_post_mix, g_pre_ffn, g_post_ffn, w_in, gla_gate_w2, gla_gate_b, diff_lambda, diff_subln, gla_onorm, w_out, ffn_w1, ffn_w3, ffn_w2, router, moe_w1, moe_w3, moe_w2):
    batch, seq, d = x.shape
    ctx_len = ctx.shape[1]
    depth = w_in.shape[0]
    n_experts = router.shape[-1]
    assert seq % ctx_len == 0 and ctx_len % GLA_CHUNK == 0 and d == 2 * HEAD_W == SUBLANES * LANES
    assert w_in.shape[-1] == _OFF_GLR + 2 * GATE_RANK
    tm = ctx_len
    tiles_per_b = (ctx_len + seq) // tm
    mod_rows = -(-(batch + 1) // SUBLANES) * SUBLANES
    geom = (tm, tiles_per_b, batch)

    cc = jnp.concatenate([c, c_ctx[None, :], jnp.zeros((mod_rows - batch - 1, d), F32)], axis=0)
    mod = _modulation(cc, w_mod, b_mod)
    mod5 = mod.reshape(depth, N_MOD, mod_rows, 1, d)
    rope_tabs = _rope_tables(seq, ctx_len)
    xall = jnp.concatenate([ctx, x], axis=1).reshape(batch * (ctx_len + seq), d)
    row2 = lambda a: a.reshape(1, -1)

    for l in range(depth):
        last = l == depth - 1
        lam_init = 0.8 - 0.6 * math.exp(-0.3 * l)
        w_main = w_in[l, :, :_OFF_GLR].astype(BF16)
        w_glr = jnp.pad(w_in[l, :, _OFF_GLR:], ((0, 0), (0, LANES - 2 * GATE_RANK))).astype(BF16)
        w2bd = jnp.zeros((LANES, 2 * GLA_KW), F32)
        w2bd = w2bd.at[:GATE_RANK, :GLA_KW].set(gla_gate_w2[l, 0])
        w2bd = w2bd.at[GATE_RANK:2 * GATE_RANK, GLA_KW:].set(gla_gate_w2[l, 1])
        w2bd = jnp.stack(_split2(w2bd))
        gate_b = gla_gate_b[l].reshape(1, 2 * GLA_KW)

        dq, dk, dv, gq, gk, gv, gg, gf, gb = _inproj(
            xall, row2(g_pre_mix[l]), mod5, l, w_main, w_glr, w2bd, gate_b, rope_tabs, geom)
        od_lat, od_ctx = _diff_attn(dq, dk, dv, diff_lambda[l], row2(diff_subln[l]), lam_init, not last,
                                    geom, batch)
        og = _gla(gq, gk, gv, gf, gb, geom, batch)

        moe = l % 2 == 1
        idx = l // 2
        router_pad = None
        if moe:
            router_pad = (jnp.pad(router[idx], ((0, 0), (0, LANES - n_experts))), n_experts)
        outs = _merge(od_lat, od_ctx, og, gg, xall, w_out[l].astype(BF16), row2(gla_onorm[l]),
                      row2(g_post_mix[l]), row2(g_pre_ffn[l]), mod5, l, router_pad, geom)
        if moe:
            xall, h_tiles, route = outs
            tile_expert, tile_valid, row_token, pos_tiles = _route_plan(route, tm, n_experts, tiles_per_b, last)
            y_tiles = _moe_ffn(h_tiles, tile_expert, tile_valid, row_token, moe_w1[idx].astype(BF16),
                               moe_w3[idx].astype(BF16), moe_w2[idx].astype(BF16), tm)
            xall = _moe_combine(y_tiles, pos_tiles, route, xall, row2(g_post_ffn[l]), mod5, l, geom, last)
        else:
            xall, h = outs
            xall = _dense_ffn(h, xall, ffn_w1[idx].astype(BF16), ffn_w3[idx].astype(BF16),
                              ffn_w2[idx].astype(BF16), row2(g_post_ffn[l]), mod5, l, geom)

    if xall.shape[0] == batch * seq:
        return xall.reshape(batch, seq, d)
    return xall.reshape(batch, ctx_len + seq, d)[:, ctx_len:, :]
```

```python
import functools
import math

import jax
import jax.numpy as jnp
from jax import lax
from jax.experimental import pallas as pl
from jax.experimental.pallas import tpu as pltpu

F32 = jnp.float32
BF16 = jnp.bfloat16
HIGHEST = lax.Precision.HIGHEST

N_HEADS = 4
QK_DIM = 64
V_DIM = 128
HEAD_W = N_HEADS * V_DIM
GLA_KW = N_HEADS * QK_DIM
GATE_RANK = 16
GATE_NORMALIZER = 16.0
GLA_CHUNK = 64
GRID_W = 64
ROPE_THETA = 10000.0
N_MOD = 6
TOP_K = 2
EPS = 1e-6
LOG2_E = 1.4426950408889634
LANES = 128
SUBLANES = 8
MXU_DIM = 256
VMEM_LIMIT = 56 * 1024 * 1024
DMA_ISSUE_UNROLL = 8
DMA_PRIORITIES = 2

_OFF_DQ, _OFF_DK, _OFF_DV = 0, 512, 1024
_OFF_GQ, _OFF_GK, _OFF_GV, _OFF_GG, _OFF_GLR = 1536, 1792, 2048, 2560, 3072


def _rms(x, g):
    ms = jnp.mean(x * x, axis=-1, keepdims=True)
    return x * lax.rsqrt(ms + EPS) * g


def _silu(x):
    return x * (1.0 / (1.0 + jnp.exp(-x)))


def _split2(a):
    hi = a.astype(BF16)
    return hi, (a - hi.astype(F32)).astype(BF16)


def _cparams(n_axes, vmem=None):
    return pltpu.CompilerParams(dimension_semantics=("arbitrary",) * n_axes,
                                vmem_limit_bytes=vmem)


def _mod_row(i, tiles_per_b, ctx_row):
    return jnp.where(i % tiles_per_b == 0, ctx_row, i // tiles_per_b)


def _to_tiles(ref, val, row0=0):
    rows = val.shape[0]
    for c in range(SUBLANES):
        ref[pl.ds(row0 * SUBLANES + c, rows, stride=SUBLANES), :] = val[:, LANES * c:LANES * (c + 1)]


def _from_tiles(ref, rows, row0=0):
    return jnp.concatenate(
        [ref[pl.ds(row0 * SUBLANES + c, rows, stride=SUBLANES), :] for c in range(SUBLANES)], axis=1)


def _mod_kernel(c_ref, w_ref, b_ref, o_ref):
    o_ref[...] = jnp.dot(_silu(c_ref[...]), w_ref[...], precision=HIGHEST,
                         preferred_element_type=F32) + b_ref[...]


def _modulation(cc, w_mod, b_mod):
    n_layers, d, _ = w_mod.shape
    r = cc.shape[0]
    b4 = b_mod.reshape(n_layers, N_MOD, 1, d)
    return pl.pallas_call(
        _mod_kernel,
        out_shape=jax.ShapeDtypeStruct((n_layers, N_MOD, r, d), F32),
        grid=(n_layers, N_MOD),
        in_specs=[pl.BlockSpec((r, d), lambda l, j: (0, 0)),
                  pl.BlockSpec((None, d, d), lambda l, j: (l, 0, j)),
                  pl.BlockSpec((None, None, 1, d), lambda l, j: (l, j, 0, 0))],
        out_specs=pl.BlockSpec((None, None, r, d), lambda l, j: (l, j, 0, 0)),
        compiler_params=_cparams(2),
        name="modulation",
    )(cc, w_mod, b4)


def _inproj_kernel(x_ref, g_ref, sc_ref, sh_ref, w_ref, wglr_ref, w2_ref, gb_ref, tri_ref,
                   cos_ref, s1_ref, s2_ref,
                   dq_ref, dk_ref, dv_ref, gq_ref, gk_ref, gv_ref, gg_ref, gf_ref, gb_out_ref):
    h = _rms(x_ref[...], g_ref[...]) * (1.0 + sc_ref[...]) + sh_ref[...]
    hb = h.astype(BF16)
    cos, s1, s2 = cos_ref[...], s1_ref[...], s2_ref[...]

    def proj(off, n):
        return jnp.dot(hb, w_ref[:, off:off + n], preferred_element_type=F32)

    def rope(p):
        outs = []
        for j in range(N_HEADS):
            xg = p[:, LANES * j:LANES * (j + 1)]
            outs.append(xg * cos + pltpu.roll(xg, LANES - 16, 1) * s1 + pltpu.roll(xg, 16, 1) * s2)
        return jnp.concatenate(outs, axis=1)

    scale = QK_DIM ** -0.5
    dq_ref[...] = (rope(proj(_OFF_DQ, HEAD_W)) * (scale * LOG2_E)).astype(BF16)
    dk_ref[...] = rope(proj(_OFF_DK, HEAD_W)).astype(BF16)
    dv_ref[...] = proj(_OFF_DV, HEAD_W).astype(BF16)
    gq_ref[...] = proj(_OFF_GQ, GLA_KW) * scale
    gk_ref[...] = proj(_OFF_GK, GLA_KW)
    gv_ref[...] = proj(_OFF_GV, HEAD_W)
    gg_ref[...] = proj(_OFF_GG, HEAD_W)
    glr = jnp.dot(hb, wglr_ref[...], preferred_element_type=F32)
    glr_hi, glr_lo = _split2(glr)
    w2_hi, w2_lo = w2_ref[0], w2_ref[1]
    z = (jnp.dot(glr_hi, w2_hi, preferred_element_type=F32) + jnp.dot(glr_hi, w2_lo, preferred_element_type=F32)
         + jnp.dot(glr_lo, w2_hi, preferred_element_type=F32)) + gb_ref[...]
    gl = (jnp.minimum(z, 0.0) - jnp.log1p(jnp.exp(-jnp.abs(z)))) * (1.0 / GATE_NORMALIZER)

    def tri_sum(tri, a):
        hi, lo = _split2(a)
        return jnp.dot(tri, hi, preferred_element_type=F32) + jnp.dot(tri, lo, preferred_element_type=F32)

    gf_ref[...] = tri_sum(tri_ref[0], gl[:, :GLA_KW])
    gb_out_ref[...] = tri_sum(tri_ref[1], gl[:, GLA_KW:])


def _chunk_tri(tm):
    r = jnp.arange(tm)[:, None]
    c = jnp.arange(tm)[None, :]
    same = (r // GLA_CHUNK) == (c // GLA_CHUNK)
    return jnp.stack([same & (r >= c), same & (c >= r)]).astype(BF16)


def _inproj(xall, g_pre, mod5, layer, w_main, w_glr, w2bd, gate_b, rope_tabs, geom):
    t, d = xall.shape
    tm, tiles_per_b, ctx_row = geom
    n_tiles = t // tm
    tri = _chunk_tri(tm)

    def mod_spec(j):
        return pl.BlockSpec((None, None, None, 1, d),
                            lambda i: (layer, j, _mod_row(i, tiles_per_b, ctx_row), 0, 0))

    row = lambda w: pl.BlockSpec((tm, w), lambda i: (i, 0))
    full = lambda a: pl.BlockSpec(a.shape, lambda i: (0,) * a.ndim)
    tab = pl.BlockSpec((tm, LANES), lambda i: (i % tiles_per_b, 0))
    widths = (HEAD_W, HEAD_W, HEAD_W, GLA_KW, GLA_KW, HEAD_W, HEAD_W, GLA_KW, GLA_KW)
    dtypes = (BF16, BF16, BF16, F32, F32, F32, F32, F32, F32)
    return pl.pallas_call(
        _inproj_kernel,
        out_shape=[jax.ShapeDtypeStruct((t, w), dt) for w, dt in zip(widths, dtypes)],
        grid=(n_tiles,),
        in_specs=[row(d), full(g_pre), mod_spec(1), mod_spec(0), full(w_main), full(w_glr),
                  full(w2bd), full(gate_b), full(tri), tab, tab, tab],
        out_specs=[row(w) for w in widths],
        compiler_params=_cparams(1, VMEM_LIMIT),
        name="inproj",
    )(xall, g_pre, mod5, mod5, w_main, w_glr, w2bd, gate_b, tri, *rope_tabs)


ATTN_Q_TILES_PER_STEP = 4


def _diff_attn_kernel(q_ref, k_ref, v_ref, lam_ref, g_ref, o_ref, *, lam_init, q_tile):
    lv = lam_ref[...]
    lam = (jnp.exp(jnp.sum(lv[0:1] * lv[1:2], axis=-1, keepdims=True))
           - jnp.exp(jnp.sum(lv[2:3] * lv[3:4], axis=-1, keepdims=True)) + lam_init)
    lane = lax.broadcasted_iota(jnp.int32, (1, LANES), 1)
    ones = jnp.ones((k_ref.shape[0], V_DIM), BF16)
    for r0 in range(0, q_ref.shape[0], q_tile):
        rows = slice(r0, r0 + q_tile)
        for hh in range(N_HEADS):
            cols = slice(V_DIM * hh, V_DIM * (hh + 1))
            q = q_ref[rows, cols]
            k = k_ref[:, cols]
            v_ext = jnp.concatenate([v_ref[:, cols], ones], axis=1)

            def one(qm):
                s = lax.dot_general(qm, k, (((1,), (1,)), ((), ())), preferred_element_type=F32)
                p = jnp.exp2(s - jnp.max(s, axis=-1, keepdims=True))
                r = jnp.dot(p.astype(BF16), v_ext, preferred_element_type=F32)
                return r[:, :V_DIM] / r[:, V_DIM:V_DIM + 1]

            o = (one(jnp.where(lane < QK_DIM, q, jnp.zeros_like(q)))
                 - lam * one(jnp.where(lane >= QK_DIM, q, jnp.zeros_like(q))))
            o_ref[rows, cols] = (_rms(o, g_ref[...]) * (1.0 - lam_init)).astype(o_ref.dtype)


def _diff_attn(dq, dk, dv, lam_vec, subln_g, lam_init, with_ctx, geom, batch):
    tm, tiles_per_b, _ = geom
    seg = tm * tiles_per_b
    seq = seg - tm
    q_tiles = math.gcd(ATTN_Q_TILES_PER_STEP, seq // tm)
    qrows = q_tiles * tm
    steps_per_b = seq // qrows
    dq3, dk3, dv3 = (a.reshape(batch, seg, HEAD_W) for a in (dq, dk, dv))
    small = [pl.BlockSpec(lam_vec.shape, lambda *_: (0, 0)), pl.BlockSpec(subln_g.shape, lambda *_: (0, 0))]
    kern = functools.partial(_diff_attn_kernel, lam_init=lam_init, q_tile=tm)
    od_lat = pl.pallas_call(
        kern,
        out_shape=jax.ShapeDtypeStruct((batch * seq, HEAD_W), BF16),
        grid=(batch, steps_per_b),
        in_specs=[pl.BlockSpec((pl.Element(qrows), pl.Element(HEAD_W)),
                               lambda b, j: (pl.multiple_of((b * tiles_per_b + 1 + q_tiles * j) * tm, tm), 0)),
                  pl.BlockSpec((None, seg, HEAD_W), lambda b, j: (b, 0, 0)),
                  pl.BlockSpec((None, seg, HEAD_W), lambda b, j: (b, 0, 0))] + small,
        out_specs=pl.BlockSpec((qrows, HEAD_W), lambda b, j: (b * steps_per_b + j, 0)),
        compiler_params=_cparams(2, VMEM_LIMIT),
        name="diff_attn",
    )(dq, dk3, dv3, lam_vec, subln_g)
    if not with_ctx:
        return od_lat, jnp.zeros((batch * tm, HEAD_W), BF16)
    ctx_spec = pl.BlockSpec((None, tm, HEAD_W), lambda b: (b, 0, 0))
    od_ctx = pl.pallas_call(
        kern,
        out_shape=jax.ShapeDtypeStruct((batch, tm, HEAD_W), BF16),
        grid=(batch,),
        in_specs=[ctx_spec, ctx_spec, ctx_spec] + small,
        out_specs=ctx_spec,
        compiler_params=_cparams(1, VMEM_LIMIT),
        name="diff_attn_ctx",
    )(dq3, dk3, dv3, lam_vec, subln_g)
    return od_lat, od_ctx.reshape(batch * tm, HEAD_W)


GLA_UNROLL = 4


def _gla_kernel(q_ref, k_ref, v_ref, gf_ref, gb_ref, o_ref, *, n_chunks, n_ctx_chunks):
    c = GLA_CHUNK
    r_i = lax.broadcasted_iota(jnp.int32, (N_HEADS * c, c), 0) % c
    c_i = lax.broadcasted_iota(jnp.int32, (N_HEADS * c, c), 1)
    lane = lax.broadcasted_iota(jnp.int32, (1, GLA_KW), 1)
    head_masks = [((lane // QK_DIM) == h).astype(F32) for h in range(N_HEADS)]
    causal = (r_i >= c_i, c_i >= r_i)
    last_row = (c - 1, 0)
    mid_row = (c // 2 - 1, c - c // 2)
    g_refs = (gf_ref, gb_ref)
    nt = (((1,), (1,)), ((), ()))
    tn = (((0,), (0,)), ((), ()))

    def head_stack(a):
        return jnp.concatenate([(a * head_masks[h]).astype(BF16) for h in range(N_HEADS)], axis=0)

    def chunk(direction, chunk_idx, st):
        rows = pl.ds(pl.multiple_of(chunk_idx * c, c), c)
        q, k, v, g = q_ref[rows, :], k_ref[rows, :], v_ref[rows, :], g_refs[direction][rows, :]
        last, mid = last_row[direction], mid_row[direction]
        g_last = g[last:last + 1, :]
        g_mid = g[mid:mid + 1, :]
        ke = (k * jnp.exp(g_mid - g)).astype(BF16)
        vb = v.astype(BF16)
        att = lax.dot_general(head_stack(q * jnp.exp(g - g_mid)), ke, nt, preferred_element_type=F32)
        att = jnp.where(causal[direction], att, 0.0).astype(BF16)
        inter = lax.dot_general(head_stack(q * jnp.exp(g)), st.astype(BF16), nt,
                                preferred_element_type=F32)
        o = jnp.concatenate(
            [jnp.dot(att[c * h:c * (h + 1), :], vb[:, V_DIM * h:V_DIM * (h + 1)], preferred_element_type=F32)
             + inter[c * h:c * (h + 1), :] for h in range(N_HEADS)], axis=1)
        o_ref[rows, :] = o_ref[rows, :] + o
        v_stack = jnp.concatenate([vb[:, V_DIM * h:V_DIM * (h + 1)] for h in range(N_HEADS)], axis=0)
        ds_t = lax.dot_general(v_stack, head_stack(k * jnp.exp(g_last - g)), tn,
                               preferred_element_type=F32)
        return st * jnp.exp(g_last) + ds_t

    o_ref[...] = jnp.zeros_like(o_ref)

    def body(i, carry):
        st_f, st_b = carry
        i_b = jnp.where(i < n_ctx_chunks, n_ctx_chunks - 1 - i, n_chunks - 1 - (i - n_ctx_chunks))
        return chunk(0, i, st_f), chunk(1, i_b, st_b)

    zero = jnp.zeros((V_DIM, GLA_KW), F32)
    lax.fori_loop(0, n_chunks, body, (zero, zero), unroll=GLA_UNROLL)


def _gla(gq, gk, gv, gf, gb, geom, batch):
    tm, tiles_per_b, _ = geom
    seg = tm * tiles_per_b
    r3 = lambda a: a.reshape(batch, seg, a.shape[-1])
    spec = lambda w: pl.BlockSpec((None, seg, w), lambda b: (b, 0, 0))
    kern = functools.partial(_gla_kernel, n_chunks=seg // GLA_CHUNK, n_ctx_chunks=tm // GLA_CHUNK)
    out = pl.pallas_call(
        kern,
        out_shape=jax.ShapeDtypeStruct((batch, seg, HEAD_W), F32),
        grid=(batch,),
        in_specs=[spec(GLA_KW), spec(GLA_KW), spec(HEAD_W), spec(GLA_KW), spec(GLA_KW)],
        out_specs=spec(HEAD_W),
        compiler_params=_cparams(1, VMEM_LIMIT),
        name="gla",
    )(r3(gq), r3(gk), r3(gv), r3(gf), r3(gb))
    return out.reshape(batch * seg, HEAD_W)


MERGE_ROW_SPLITS = 1


def _merge_kernel(od_lat_ref, od_ctx_ref, og_ref, gg_ref, x_ref, wout_ref, onorm_ref, gpost_ref, gt_ref,
                  gpre_ref, sc_ref, sh_ref, *rest, n_experts, tiles_per_b):
    if n_experts:
        router_ref, xo_ref, h_ref, route_ref = rest
    else:
        xo_ref, h_ref = rest
    onorm = onorm_ref[...]
    is_ctx = pl.program_id(0) % tiles_per_b == 0
    n_rows = x_ref.shape[0] // MERGE_ROW_SPLITS
    for r0 in range(0, x_ref.shape[0], n_rows):
        rs = pl.ds(r0, n_rows)
        og = og_ref[rs, :]
        ogn = jnp.concatenate([_rms(og[:, V_DIM * h:V_DIM * (h + 1)], onorm) for h in range(N_HEADS)], axis=1)
        ogn = (ogn * _silu(gg_ref[rs, :])).astype(BF16)
        od = jnp.where(is_ctx, od_ctx_ref[rs, :], od_lat_ref[rs, :])
        y = (jnp.dot(od, wout_ref[0:HEAD_W, :], preferred_element_type=F32)
             + jnp.dot(ogn, wout_ref[HEAD_W:2 * HEAD_W, :], preferred_element_type=F32))
        x_new = x_ref[rs, :] + gt_ref[...] * _rms(y, gpost_ref[...])
        xo_ref[rs, :] = x_new
        h = _rms(x_new, gpre_ref[...]) * (1.0 + sc_ref[...]) + sh_ref[...]
        if not n_experts:
            h_ref[rs, :] = h.astype(h_ref.dtype)
            continue
        _to_tiles(h_ref, h, r0)
        h_hi, h_lo = _split2(h)
        r_hi, r_lo = _split2(router_ref[...])
        logits = (jnp.dot(h_hi, r_hi, preferred_element_type=F32) + jnp.dot(h_hi, r_lo, preferred_element_type=F32)
                  + jnp.dot(h_lo, r_hi, preferred_element_type=F32))
        lane = lax.broadcasted_iota(jnp.int32, logits.shape, 1)
        neg = jnp.float32(-jnp.inf)
        lg = jnp.where(lane < n_experts, logits, neg)
        m1 = jnp.max(lg, axis=-1, keepdims=True)
        i1 = jnp.min(jnp.where(lg == m1, lane, LANES), axis=-1, keepdims=True)
        lg2 = jnp.where(lane == i1, neg, lg)
        m2 = jnp.max(lg2, axis=-1, keepdims=True)
        i2 = jnp.min(jnp.where(lg2 == m2, lane, LANES), axis=-1, keepdims=True)
        e2 = jnp.exp(m2 - m1)
        w1 = 1.0 / (1.0 + e2)
        w2 = e2 / (1.0 + e2)
        route_ref[rs, :] = jnp.where(lane == 0, i1.astype(F32),
                                     jnp.where(lane == 1, i2.astype(F32),
                                               jnp.where(lane == 2, w1, jnp.where(lane == 3, w2, 0.0))))


def _merge(od_lat, od_ctx, og, gg, xall, w_out, onorm, g_post, g_pre_ffn, mod5, layer, router_pad, geom):
    t, d = xall.shape
    tm, tiles_per_b, ctx_row = geom
    n_experts = 0 if router_pad is None else router_pad[1]
    lat_tiles = tiles_per_b - 1

    def mod_spec(j):
        return pl.BlockSpec((None, None, None, 1, d),
                            lambda i: (layer, j, _mod_row(i, tiles_per_b, ctx_row), 0, 0))

    row = lambda w: pl.BlockSpec((tm, w), lambda i: (i, 0))
    full = lambda a: pl.BlockSpec(a.shape, lambda i: (0,) * a.ndim)
    od_lat_spec = pl.BlockSpec(
        (tm, HEAD_W), lambda i: ((i // tiles_per_b) * lat_tiles + jnp.maximum(i % tiles_per_b - 1, 0), 0))
    od_ctx_spec = pl.BlockSpec((tm, HEAD_W), lambda i: (i // tiles_per_b, 0))
    in_specs = [od_lat_spec, od_ctx_spec, row(HEAD_W), row(HEAD_W), row(d), full(w_out), full(onorm),
                full(g_post), mod_spec(2), full(g_pre_ffn), mod_spec(4), mod_spec(3)]
    args = [od_lat, od_ctx, og, gg, xall, w_out, onorm, g_post, mod5, g_pre_ffn, mod5, mod5]
    out_shape = [jax.ShapeDtypeStruct((t, d), F32)]
    out_specs = [row(d)]
    if n_experts:
        in_specs.append(full(router_pad[0]))
        args.append(router_pad[0])
        out_shape += [jax.ShapeDtypeStruct((t * SUBLANES, LANES), F32), jax.ShapeDtypeStruct((t, LANES), F32)]
        out_specs += [pl.BlockSpec((tm * SUBLANES, LANES), lambda i: (i, 0)), row(LANES)]
    else:
        out_shape.append(jax.ShapeDtypeStruct((t, d), BF16))
        out_specs.append(row(d))
    return pl.pallas_call(
        functools.partial(_merge_kernel, n_experts=n_experts, tiles_per_b=tiles_per_b),
        out_shape=out_shape,
        grid=(t // tm,),
        in_specs=in_specs,
        out_specs=out_specs,
        compiler_params=_cparams(1, VMEM_LIMIT),
        name="merge_outproj",
    )(*args)


def _swiglu(xb, w1_ref, w3_ref, w2_ref):
    d_ff = w1_ref.shape[-1]
    cut = -(-(d_ff // 2) // MXU_DIM) * MXU_DIM
    y = None
    for c0, c1 in ((0, cut), (cut, d_ff)):
        a = jnp.dot(xb, w1_ref[:, c0:c1], preferred_element_type=F32)
        b = jnp.dot(xb, w3_ref[:, c0:c1], preferred_element_type=F32)
        m = (_silu(a) * b).astype(BF16)
        part = jnp.dot(m, w2_ref[c0:c1, :], preferred_element_type=F32)
        y = part if y is None else y + part
    return y


def _dense_ffn_kernel(h_ref, x_ref, w1_ref, w3_ref, w2_ref, gpost_ref, gt_ref, o_ref):
    y = _swiglu(h_ref[...], w1_ref, w3_ref, w2_ref)
    o_ref[...] = x_ref[...] + gt_ref[...] * _rms(y, gpost_ref[...])


def _dense_ffn(h, xall, w1, w3, w2, g_post, mod5, layer, geom):
    t, d = xall.shape
    tm, tiles_per_b, ctx_row = geom
    row = lambda w: pl.BlockSpec((tm, w), lambda i: (i, 0))
    resident = lambda a: pl.BlockSpec(a.shape, lambda i: (0,) * a.ndim, pipeline_mode=pl.Buffered(1))
    gt_spec = pl.BlockSpec((None, None, None, 1, d),
                           lambda i: (layer, 5, _mod_row(i, tiles_per_b, ctx_row), 0, 0))
    return pl.pallas_call(
        _dense_ffn_kernel,
        out_shape=jax.ShapeDtypeStruct((t, d), F32),
        grid=(t // tm,),
        in_specs=[row(d), row(d), resident(w1), resident(w3), resident(w2),
                  pl.BlockSpec(g_post.shape, lambda i: (0, 0)), gt_spec],
        out_specs=row(d),
        compiler_params=_cparams(1, VMEM_LIMIT),
        name="dense_ffn",
    )(h, xall, w1, w3, w2, g_post, mod5)


def _tile_copy(src, src_tok, dst, dst_tok, sem):
    s0 = pl.multiple_of(src_tok * SUBLANES, SUBLANES)
    d0 = pl.multiple_of(dst_tok * SUBLANES, SUBLANES)
    return pltpu.make_async_copy(src.at[pl.ds(s0, SUBLANES), :], dst.at[pl.ds(d0, SUBLANES), :], sem)


GATHER_SLOTS = 3


def _gather_step(src_hbm, idx_refs, buf, sem, n_copies, compute, valid=None, skipped=None):
    idx_ref, idx_next_ref, idx_next2_ref = idx_refs
    i = pl.program_id(0)
    slot = lax.rem(i, GATHER_SLOTS)
    slot1 = lax.rem(i + 1, GATHER_SLOTS)
    slot2 = lax.rem(i + 2, GATHER_SLOTS)

    def start(ref, s, r, priority=0):
        _tile_copy(src_hbm, ref[0, 0, r], buf.at[s], r, sem.at[s]).start(priority=priority)

    def issue_loop(ref, s):
        lax.fori_loop(0, n_copies, lambda r, c: (start(ref, s, r), c)[1], 0, unroll=DMA_ISSUE_UNROLL)

    def wait_slot(s):
        def wait(r, carry):
            _tile_copy(src_hbm, 0, buf.at[s], r, sem.at[s]).wait()
            return carry
        lax.fori_loop(0, n_copies, wait, 0, unroll=DMA_ISSUE_UNROLL)

    def compute_and_issue():
        compute(buf.at[slot])
        for r in range(n_copies):
            start(idx_next2_ref, slot2, r, priority=r % DMA_PRIORITIES)

    @pl.when(i == 0)
    def _():
        issue_loop(idx_ref, 0)
        issue_loop(idx_next_ref, 1)

    wait_slot(slot)
    if valid is None:
        compute_and_issue()
    else:
        pl.when(valid)(compute_and_issue)

        @pl.when(jnp.logical_not(valid))
        def _():
            skipped()
            issue_loop(idx_next2_ref, slot2)

    @pl.when(i == pl.num_programs(0) - 1)
    def _():
        wait_slot(slot1)
        wait_slot(slot2)


def _moe_ffn_kernel(te_ref, tv_ref, rows_ref, rows1_ref, rows2_ref, h_hbm, w1_ref, w3_ref, w2_ref, y_ref,
                    xbuf, sem, *, tm):
    del te_ref

    def compute(x_tiles):
        _to_tiles(y_ref, _swiglu(_from_tiles(x_tiles, tm).astype(BF16), w1_ref, w3_ref, w2_ref))

    def skipped():
        y_ref[...] = jnp.zeros_like(y_ref)

    _gather_step(h_hbm, (rows_ref, rows1_ref, rows2_ref), xbuf, sem, tm, compute,
                 valid=tv_ref[pl.program_id(0)] == 1, skipped=skipped)


def _ahead_specs(block, n_steps, tile=lambda i: i):
    def spec(k):
        return pl.BlockSpec((1, 1, block), lambda i, *_: (tile(jnp.minimum(i + k, n_steps - 1)), 0, 0),
                            memory_space=pltpu.SMEM)
    return [spec(k) for k in range(GATHER_SLOTS)]


def _moe_ffn(h_tiles, tile_expert, tile_valid, row_token, w1, w3, w2, tm):
    n_tiles = tile_expert.shape[0]
    _, d, d_ff = w1.shape
    rows = row_token.reshape(n_tiles, 1, tm)
    grid_spec = pltpu.PrefetchScalarGridSpec(
        num_scalar_prefetch=2,
        grid=(n_tiles,),
        in_specs=_ahead_specs(tm, n_tiles) + [
            pl.BlockSpec(memory_space=pl.ANY),
            pl.BlockSpec((None, d, d_ff), lambda i, te, tv: (te[i], 0, 0)),
            pl.BlockSpec((None, d, d_ff), lambda i, te, tv: (te[i], 0, 0)),
            pl.BlockSpec((None, d_ff, d), lambda i, te, tv: (te[i], 0, 0))],
        out_specs=pl.BlockSpec((tm * SUBLANES, LANES), lambda i, te, tv: (i, 0)),
        scratch_shapes=[pltpu.VMEM((GATHER_SLOTS, tm * SUBLANES, LANES), F32),
                        pltpu.SemaphoreType.DMA((GATHER_SLOTS,))],
    )
    return pl.pallas_call(
        functools.partial(_moe_ffn_kernel, tm=tm),
        out_shape=jax.ShapeDtypeStruct((n_tiles * tm * SUBLANES, LANES), F32),
        grid_spec=grid_spec,
        compiler_params=_cparams(1, VMEM_LIMIT),
        name="moe_ffn",
    )(tile_expert, tile_valid, rows, rows, rows, h_tiles, w1, w3, w2)


def _moe_combine_kernel(pos_ref, pos1_ref, pos2_ref, y_hbm, route_ref, x_ref, gpost_ref, gt_ref, o_ref,
                        ybuf, sem):
    tm = x_ref.shape[0]

    def compute(y_tiles):
        route = route_ref[...]
        y = route[:, 2:3] * _from_tiles(y_tiles, tm) + route[:, 3:4] * _from_tiles(y_tiles, tm, tm)
        o_ref[...] = x_ref[...] + gt_ref[...] * _rms(y, gpost_ref[...])

    _gather_step(y_hbm, (pos_ref, pos1_ref, pos2_ref), ybuf, sem, TOP_K * tm, compute)


def _moe_combine(y_tiles, pos_tiles, route, xall, g_post, mod5, layer, geom, latent_only):
    t, d = xall.shape
    tm, tiles_per_b, ctx_row = geom
    if latent_only:
        lat_tiles = tiles_per_b - 1
        n_steps = (t // tm // tiles_per_b) * lat_tiles
        tile = lambda i: (i // lat_tiles) * tiles_per_b + 1 + i % lat_tiles
    else:
        n_steps = t // tm
        tile = lambda i: i
    row = lambda w: pl.BlockSpec((tm, w), lambda i: (tile(i), 0))
    gt_spec = pl.BlockSpec((None, None, None, 1, d),
                           lambda i: (layer, 5, _mod_row(tile(i), tiles_per_b, ctx_row), 0, 0))
    return pl.pallas_call(
        _moe_combine_kernel,
        out_shape=jax.ShapeDtypeStruct((n_steps * tm, d), F32),
        grid=(n_steps,),
        in_specs=_ahead_specs(TOP_K * tm, n_steps, tile) + [
            pl.BlockSpec(memory_space=pl.ANY),
            row(LANES), row(d), pl.BlockSpec(g_post.shape, lambda i: (0, 0)), gt_spec],
        out_specs=pl.BlockSpec((tm, d), lambda i: (i, 0)),
        scratch_shapes=[pltpu.VMEM((GATHER_SLOTS, TOP_K * tm * SUBLANES, LANES), F32),
                        pltpu.SemaphoreType.DMA((GATHER_SLOTS,))],
        compiler_params=_cparams(1, VMEM_LIMIT),
        name="moe_combine",
    )(pos_tiles, pos_tiles, pos_tiles, y_tiles, route, xall, g_post, mod5)


def _route_plan(route, tm, n_experts, tiles_per_b, latent_only):
    t = route.shape[0]
    e_tk = route[:, :TOP_K].astype(jnp.int32)
    n_routed = t
    if latent_only:
        is_ctx = (jnp.arange(t, dtype=jnp.int32) // tm) % tiles_per_b == 0
        e_tk = jnp.where(is_ctx[:, None], n_experts, e_tk)
        n_routed = t // tiles_per_b * (tiles_per_b - 1)
    n_tiles = (TOP_K * n_routed) // tm + n_experts
    e_flat = e_tk.T.reshape(-1)
    onehot = (e_flat[:, None] == jnp.arange(n_experts, dtype=jnp.int32)[None, :]).astype(jnp.int32)
    csum = jnp.cumsum(onehot, axis=0)
    counts = csum[-1]
    rank = jnp.sum(csum * onehot, axis=1) - 1
    group_tiles = (counts + tm - 1) // tm
    group_end = jnp.cumsum(group_tiles)
    group_start = (group_end - group_tiles) * tm
    pos = jnp.sum(onehot * group_start[None, :], axis=1) + rank
    tile_group = jnp.sum((jnp.arange(n_tiles, dtype=jnp.int32)[:, None] >= group_end[None, :]).astype(jnp.int32), axis=1)
    tile_valid = (tile_group < n_experts).astype(jnp.int32)
    tile_expert = jnp.minimum(tile_group, n_experts - 1)
    n_rows = n_tiles * tm
    n_assign = TOP_K * t
    token = jnp.tile(jnp.arange(t, dtype=jnp.int32), TOP_K)
    routed = e_flat < n_experts
    assign_key = jnp.where(routed, pos, n_rows + jnp.arange(n_assign, dtype=jnp.int32))
    fill = jnp.arange(n_rows - TOP_K * n_routed, dtype=jnp.int32)
    pad = group_tiles * tm - counts
    pad_end = jnp.cumsum(pad)
    open_ = (fill[:, None] < pad_end[None, :]).astype(jnp.int32)
    own = open_ - jnp.concatenate([jnp.zeros_like(open_[:, :1]), open_[:, :-1]], axis=1)
    base = group_start + counts - (pad_end - pad)
    fill_key = jnp.where(open_[:, -1] == 1, jnp.sum(own * base[None, :], axis=1),
                         group_end[-1] * tm - pad_end[-1]) + fill
    keys = jnp.concatenate([assign_key, fill_key])
    vals = jnp.concatenate([token, jnp.zeros_like(fill)])
    row_token = lax.sort((keys, vals), num_keys=1)[1][:n_rows]
    pos = jnp.maximum(pos, 0)
    pos_tiles = pos.reshape(TOP_K, t // tm, tm).transpose(1, 0, 2).reshape(t // tm, 1, TOP_K * tm)
    return tile_expert, tile_valid, row_token, pos_tiles


def _rope_tables(seq, ctx_len):
    rows_n = seq // GRID_W
    row = jnp.repeat(jnp.arange(rows_n), GRID_W).astype(F32)
    col = jnp.tile(jnp.arange(GRID_W), rows_n).astype(F32)
    n_freq = QK_DIM // 4
    inv = ROPE_THETA ** (-jnp.arange(n_freq, dtype=F32) / n_freq)
    ang = jnp.concatenate([row[:, None] * inv, row[:, None] * inv, col[:, None] * inv, col[:, None] * inv], axis=1)
    cos, sin = jnp.cos(ang), jnp.sin(ang)
    first = (jnp.arange(QK_DIM) % (2 * n_freq)) < n_freq
    s1 = jnp.where(first[None, :], -sin, 0.0)
    s2 = jnp.where(first[None, :], 0.0, sin)
    pad = lambda a, v: jnp.concatenate([jnp.full((ctx_len, LANES), v, F32), jnp.tile(a, (1, 2))], axis=0)
    return pad(cos, 1.0), pad(s1, 0.0), pad(s2, 0.0)


def kernel(x, c, ctx, c_ctx, w_mod, b_mod, g_pre_mix, g_post_mix, g_pre_ffn, g_post_ffn, w_in, gla_gate_w2, gla_gate_b, diff_lambda, diff_subln, gla_onorm, w_out, ffn_w1, ffn_w3, ffn_w2, router, moe_w1, moe_w3, moe_w2):
    batch, seq, d = x.shape
    ctx_len = ctx.shape[1]
    depth = w_in.shape[0]
    n_experts = router.shape[-1]
    assert seq % ctx_len == 0 and ctx_len % GLA_CHUNK == 0 and d == 2 * HEAD_W == SUBLANES * LANES
    assert w_in.shape[-1] == _OFF_GLR + 2 * GATE_RANK
    tm = ctx_len
    tiles_per_b = (ctx_len + seq) // tm
    mod_rows = -(-(batch + 1) // SUBLANES) * SUBLANES
    geom = (tm, tiles_per_b, batch)

    cc = jnp.concatenate([c, c_ctx[None, :], jnp.zeros((mod_rows - batch - 1, d), F32)], axis=0)
    mod = _modulation(cc, w_mod, b_mod)
    mod5 = mod.reshape(depth, N_MOD, mod_rows, 1, d)
    rope_tabs = _rope_tables(seq, ctx_len)
    xall = jnp.concatenate([ctx, x], axis=1).reshape(batch * (ctx_len + seq), d)
    row2 = lambda a: a.reshape(1, -1)

    for l in range(depth):
        last = l == depth - 1
        lam_init = 0.8 - 0.6 * math.exp(-0.3 * l)
        w_main = w_in[l, :, :_OFF_GLR].astype(BF16)
        w_glr = jnp.pad(w_in[l, :, _OFF_GLR:], ((0, 0), (0, LANES - 2 * GATE_RANK))).astype(BF16)
        w2bd = jnp.zeros((LANES, 2 * GLA_KW), F32)
        w2bd = w2bd.at[:GATE_RANK, :GLA_KW].set(gla_gate_w2[l, 0])
        w2bd = w2bd.at[GATE_RANK:2 * GATE_RANK, GLA_KW:].set(gla_gate_w2[l, 1])
        w2bd = jnp.stack(_split2(w2bd))
        gate_b = gla_gate_b[l].reshape(1, 2 * GLA_KW)

        dq, dk, dv, gq, gk, gv, gg, gf, gb = _inproj(
            xall, row2(g_pre_mix[l]), mod5, l, w_main, w_glr, w2bd, gate_b, rope_tabs, geom)
        od_lat, od_ctx = _diff_attn(dq, dk, dv, diff_lambda[l], row2(diff_subln[l]), lam_init, not last,
                                    geom, batch)
        og = _gla(gq, gk, gv, gf, gb, geom, batch)

        moe = l % 2 == 1
        idx = l // 2
        router_pad = None
        if moe:
            router_pad = (jnp.pad(router[idx], ((0, 0), (0, LANES - n_experts))), n_experts)
        outs = _merge(od_lat, od_ctx, og, gg, xall, w_out[l].astype(BF16), row2(gla_onorm[l]),
                      row2(g_post_mix[l]), row2(g_pre_ffn[l]), mod5, l, router_pad, geom)
        if moe:
            xall, h_tiles, route = outs
            tile_expert, tile_valid, row_token, pos_tiles = _route_plan(route, tm, n_experts, tiles_per_b, last)
            y_tiles = _moe_ffn(h_tiles, tile_expert, tile_valid, row_token, moe_w1[idx].astype(BF16),
                               moe_w3[idx].astype(BF16), moe_w2[idx].astype(BF16), tm)
            xall = _moe_combine(y_tiles, pos_tiles, route, xall, row2(g_post_ffn[l]), mod5, l, geom, last)
        else:
            xall, h = outs
            xall = _dense_ffn(h, xall, ffn_w1[idx].astype(BF16), ffn_w3[idx].astype(BF16),
                              ffn_w2[idx].astype(BF16), row2(g_post_ffn[l]), mod5, l, geom)

    if xall.shape[0] == batch * seq:
        return xall.reshape(batch, seq, d)
    return xall.reshape(batch, ctx_len + seq, d)[:, ctx_len:, :]
```
